```python
import math
import jax, jax.numpy as jnp
from jax import lax
import numpy as np

D_MODEL = 4096
BATCH = 1
SEQ = 16384
DEPTH = 1
DEC_BATCH = 8
DEC_SEQ = 16
PAST_LEN = 2048

CHUNK = 64
Q_BLOCK = 128
HEAD_DIM = 128
N_HEADS_A = D_MODEL // 256
N_KV_A = 4
N_IDX_HEADS = 16
IDX_DIM = 64
TOPK_MAX = 256
N_HEADS_B = D_MODEL // 512
DB = 2 * HEAD_DIM
D_MIX = N_HEADS_A * HEAD_DIM + N_HEADS_B * DB
D_FF = ((8 * D_MODEL // 3 + 255) // 256) * 256
ROPE_THETA = 500000.0
ROT_FRAC = 4
EPS = 1e-6
N_MOD = 9
IN_WIDTHS = (N_HEADS_A * HEAD_DIM, N_KV_A * HEAD_DIM, N_KV_A * HEAD_DIM,
             N_IDX_HEADS * IDX_DIM, IDX_DIM, N_IDX_HEADS,
             N_HEADS_B * DB, N_HEADS_B * DB, N_HEADS_B * DB)
D_IN = 2048 + 512 + 512 + 1024 + 64 + 16 + 3 * 2048

kernel_name = "hybrid_dsa_diffattn_macaron_stream_step"


def lambda_init(layer_idx):
    return 0.8 - 0.6 * math.exp(-0.3 * layer_idx)


def rmsnorm(x, g):
    xf = x.astype(jnp.float32)
    y = xf * lax.rsqrt(jnp.mean(xf * xf, axis=-1, keepdims=True) + EPS)
    return (y * g.astype(jnp.float32)).astype(x.dtype)


def rope(x, pos):
    d = x.shape[-1]
    r = d // ROT_FRAC
    half = r // 2
    inv = ROPE_THETA ** (-(jnp.arange(half, dtype=jnp.float32) * 2.0 / r))
    ang = pos.astype(jnp.float32)[:, None] * inv[None, :]
    shp = (pos.shape[0],) + (1,) * (x.ndim - 3) + (half,)
    cos = jnp.cos(ang).reshape(shp)
    sin = jnp.sin(ang).reshape(shp)
    xf = x.astype(jnp.float32)
    x1 = xf[..., :half]
    x2 = xf[..., half:r]
    out = jnp.concatenate([x1 * cos - x2 * sin, x2 * cos + x1 * sin, xf[..., r:]], axis=-1)
    return out.astype(x.dtype)


def modulate(x, g, shift, scale):
    return rmsnorm(x, g) * (1.0 + scale[:, None, :]) + shift[:, None, :]


def swiglu(h, w1, w3, w2):
    return (jax.nn.silu(h @ w1) * (h @ w3)) @ w2


def to_blocks(a):
    B, S = a.shape[:2]
    return jnp.moveaxis(a.reshape((B, S // Q_BLOCK, Q_BLOCK) + a.shape[2:]), 1, 0)


def from_blocks(a):
    nb, B, qb = a.shape[:3]
    return jnp.moveaxis(a, 0, 1).reshape((B, nb * qb) + a.shape[3:])


def project(h, pos, w_in, qn_a, kn_a, kn_i, qn_b, kn_b):
    B, S, _ = h.shape
    z = h @ w_in
    offs = np.cumsum(IN_WIDTHS)[:-1].tolist()
    q_a, k_a, v_a, q_i, k_i, w_i, q_b, k_b, v_b = jnp.split(z, offs, axis=-1)
    q_a = rope(rmsnorm(q_a.reshape(B, S, N_HEADS_A, HEAD_DIM), qn_a), pos)
    k_a = rope(rmsnorm(k_a.reshape(B, S, N_KV_A, HEAD_DIM), kn_a), pos)
    v_a = v_a.reshape(B, S, N_KV_A, HEAD_DIM)
    q_i = rope(q_i.reshape(B, S, N_IDX_HEADS, IDX_DIM), pos)
    k_i = rope(rmsnorm(k_i, kn_i)[:, :, None, :], pos)[:, :, 0, :]
    q_b = rope(rmsnorm(q_b.reshape(B, S, N_HEADS_B, 2, HEAD_DIM), qn_b), pos)
    k_b = rope(rmsnorm(k_b.reshape(B, S, N_HEADS_B, 2, HEAD_DIM), kn_b), pos)
    v_b = v_b.reshape(B, S, N_HEADS_B, DB)
    return q_a, k_a, v_a, q_i, k_i, w_i, q_b, k_b, v_b


def dsa_block(q, q_idx, w_idx, q_pos, k_all, v_all, kidx_all, k_pos, topk):
    B, Q = q.shape[:2]
    L = k_pos.shape[0]
    allowed = (k_pos[None, :] // CHUNK) <= (q_pos[:, None] // CHUNK)
    logits = jnp.einsum('bqhe,ble->bqhl', q_idx.astype(jnp.float32),
                        kidx_all.astype(jnp.float32)) * (IDX_DIM ** -0.5)
    score = jnp.einsum('bqh,bqhl->bql', w_idx.astype(jnp.float32) * (N_IDX_HEADS ** -0.5),
                       jax.nn.relu(logits))
    score = jnp.where(allowed[None], score, -jnp.inf)
    _, idx = lax.top_k(score, topk)
    gather = jax.vmap(lambda kk, ii: kk[ii])
    k_sel = gather(k_all, idx).astype(jnp.float32)
    v_sel = gather(v_all, idx).astype(jnp.float32)
    ok = jnp.take_along_axis(jnp.broadcast_to(allowed[None], (B, Q, L)), idx, axis=2)
    G = N_HEADS_A // N_KV_A
    qg = q.reshape(B, Q, N_KV_A, G, HEAD_DIM).astype(jnp.float32)
    s = jnp.einsum('bqkgd,bqnkd->bqkgn', qg, k_sel) * (HEAD_DIM ** -0.5)
    s = jnp.where(ok[:, :, None, None, :], s, -jnp.inf)
    p = jax.nn.softmax(s, axis=-1)
    o = jnp.einsum('bqkgn,bqnkd->bqkgd', p, v_sel)
    return o.reshape(B, Q, N_HEADS_A * HEAD_DIM).astype(q.dtype)


def diff_block(q, q_pos, k, v, k_pos, lam):
    allowed = (k_pos[None, :] // CHUNK) <= (q_pos[:, None] // CHUNK)
    s = jnp.einsum('bqhcd,blhcd->bhcql', q.astype(jnp.float32),
                   k.astype(jnp.float32)) * (HEAD_DIM ** -0.5)
    s = jnp.where(allowed[None, None, None], s, -jnp.inf)
    p = jax.nn.softmax(s, axis=-1)
    a = p[:, :, 0] - lam * p[:, :, 1]
    return jnp.einsum('bhql,blhe->bqhe', a, v.astype(jnp.float32))


def token_mix(h, pos, past, w_in, qn_a, kn_a, kn_i, qn_b, kn_b,
              lam_q1, lam_k1, lam_q2, lam_k2, subln_g, w_out, li):
    B, S, _ = h.shape
    q_a, k_a, v_a, q_i, k_i, w_i, q_b, k_b, v_b = project(h, pos, w_in, qn_a, kn_a, kn_i, qn_b, kn_b)
    new_rows = (k_a, v_a, k_i, k_b, v_b)
    if past is None:
        keys = new_rows
        k_pos = pos
    else:
        keys = tuple(jnp.concatenate([c.astype(n.dtype), n], axis=1) for c, n in zip(past, new_rows))
        k_pos = jnp.arange(past[0].shape[1] + S, dtype=jnp.int32)
    K_a, V_a, K_i, K_b, V_b = keys
    topk = min(TOPK_MAX, k_pos.shape[0] // 4)
    lam = (jnp.exp(jnp.sum(lam_q1.astype(jnp.float32) * lam_k1.astype(jnp.float32)))
           - jnp.exp(jnp.sum(lam_q2.astype(jnp.float32) * lam_k2.astype(jnp.float32)))
           + lambda_init(li))

    def attend(qa, qi, wi, qb, qp):
        oa = dsa_block(qa, qi, wi, qp, K_a, V_a, K_i, k_pos, topk)
        ob = diff_block(qb, qp, K_b, V_b, k_pos, lam)
        return oa, ob

    if S > Q_BLOCK and S % Q_BLOCK == 0:
        oa, ob = lax.map(lambda xs: attend(*xs),
                         (to_blocks(q_a), to_blocks(q_i), to_blocks(w_i), to_blocks(q_b),
                          pos.reshape(-1, Q_BLOCK)))
        oa, ob = from_blocks(oa), from_blocks(ob)
    else:
        oa, ob = attend(q_a, q_i, w_i, q_b, pos)
    ob = rmsnorm(ob, subln_g) * (1.0 - lambda_init(li))
    o = jnp.concatenate([oa, ob.reshape(B, S, N_HEADS_B * DB).astype(oa.dtype)], axis=-1)
    return o @ w_out, new_rows


def layer(x, c, pos, past, lp, li):
    (w_ada, b_ada, norm_ffn1, w1_ffn1, w3_ffn1, w2_ffn1, norm_attn, w_in,
     qn_a, kn_a, kn_i, qn_b, kn_b, lam_q1, lam_k1, lam_q2, lam_k2, subln_g, w_out,
     norm_ffn2, w1_ffn2, w3_ffn2, w2_ffn2) = lp
    B = x.shape[0]
    mod = (jax.nn.silu(c) @ w_ada + b_ada).reshape(B, N_MOD, D_MODEL)
    sh1, sc1, gt1, sh2, sc2, gt2, sh3, sc3, gt3 = [mod[:, i] for i in range(N_MOD)]
    h = modulate(x, norm_ffn1, sh1, sc1)
    x = x + 0.5 * gt1[:, None, :] * swiglu(h, w1_ffn1, w3_ffn1, w2_ffn1)
    h = modulate(x, norm_attn, sh2, sc2)
    y, rows = token_mix(h, pos, past, w_in, qn_a, kn_a, kn_i, qn_b, kn_b,
                        lam_q1, lam_k1, lam_q2, lam_k2, subln_g, w_out, li)
    x = x + gt2[:, None, :] * y
    h = modulate(x, norm_ffn2, sh3, sc3)
    x = x + 0.5 * gt3[:, None, :] * swiglu(h, w1_ffn2, w3_ffn2, w2_ffn2)
    return x, rows


def setup_inputs(seed: int = 0) -> dict:
    key = jax.random.key(seed)
    ks = jax.random.split(key, 40)
    f32 = jnp.float32
    D = D_MODEL

    def nrm(k, shape, scale):
        return jax.random.normal(k, shape, f32) * scale

    def gain(k, shape):
        return 1.0 + 0.05 * jax.random.normal(k, shape, f32)

    return {
        "x_prompt": nrm(ks[0], (BATCH, SEQ, D), 1.0),
        "x_sample": nrm(ks[1], (DEC_BATCH, DEC_SEQ, D), 1.0),
        "cache_k_a": nrm(ks[2], (DEPTH, DEC_BATCH, PAST_LEN, N_KV_A, HEAD_DIM), 1.0),
        "cache_v_a": nrm(ks[3], (DEPTH, DEC_BATCH, PAST_LEN, N_KV_A, HEAD_DIM), 1.0),
        "cache_k_idx": nrm(ks[4], (DEPTH, DEC_BATCH, PAST_LEN, IDX_DIM), 1.0),
        "cache_k_b": nrm(ks[5], (DEPTH, DEC_BATCH, PAST_LEN, N_HEADS_B, 2, HEAD_DIM), 1.0),
        "cache_v_b": nrm(ks[6], (DEPTH, DEC_BATCH, PAST_LEN, N_HEADS_B, DB), 1.0),
        "c_prompt": nrm(ks[7], (BATCH, D), 1.0),
        "c_sample": nrm(ks[8], (DEC_BATCH, D), 1.0),
        "w_ada": nrm(ks[9], (DEPTH, D, N_MOD * D), D ** -0.5),
        "b_ada": nrm(ks[10], (DEPTH, N_MOD * D), 0.02),
        "norm_ffn1": gain(ks[11], (DEPTH, D)),
        "w1_ffn1": nrm(ks[12], (DEPTH, D, D_FF), D ** -0.5),
        "w3_ffn1": nrm(ks[13], (DEPTH, D, D_FF), D ** -0.5),
        "w2_ffn1": nrm(ks[14], (DEPTH, D_FF, D), D_FF ** -0.5),
        "norm_attn": gain(ks[15], (DEPTH, D)),
        "w_in": nrm(ks[16], (DEPTH, D, D_IN), D ** -0.5),
        "qnorm_a": gain(ks[17], (DEPTH, HEAD_DIM)),
        "knorm_a": gain(ks[18], (DEPTH, HEAD_DIM)),
        "knorm_idx": gain(ks[19], (DEPTH, IDX_DIM)),
        "qnorm_b": gain(ks[20], (DEPTH, HEAD_DIM)),
        "knorm_b": gain(ks[21], (DEPTH, HEAD_DIM)),
        "lam_q1": nrm(ks[22], (DEPTH, HEAD_DIM), 0.1),
        "lam_k1": nrm(ks[23], (DEPTH, HEAD_DIM), 0.1),
        "lam_q2": nrm(ks[24], (DEPTH, HEAD_DIM), 0.1),
        "lam_k2": nrm(ks[25], (DEPTH, HEAD_DIM), 0.1),
        "subln_b": gain(ks[26], (DEPTH, DB)),
        "w_out": nrm(ks[27], (DEPTH, D_MIX, D), D_MIX ** -0.5),
        "norm_ffn2": gain(ks[28], (DEPTH, D)),
        "w1_ffn2": nrm(ks[29], (DEPTH, D, D_FF), D ** -0.5),
        "w3_ffn2": nrm(ks[30], (DEPTH, D, D_FF), D ** -0.5),
        "w2_ffn2": nrm(ks[31], (DEPTH, D_FF, D), D_FF ** -0.5),
    }


def reference(x_prompt, x_sample, cache_k_a, cache_v_a, cache_k_idx, cache_k_b, cache_v_b,
              c_prompt, c_sample, w_ada, b_ada, norm_ffn1, w1_ffn1, w3_ffn1, w2_ffn1,
              norm_attn, w_in, qnorm_a, knorm_a, knorm_idx, qnorm_b, knorm_b,
              lam_q1, lam_k1, lam_q2, lam_k2, subln_b, w_out,
              norm_ffn2, w1_ffn2, w3_ffn2, w2_ffn2):
    S = x_prompt.shape[1]
    P = cache_k_a.shape[2]
    S_dec = x_sample.shape[1]
    pos_p = jnp.arange(S, dtype=jnp.int32)
    pos_s = P + jnp.arange(S_dec, dtype=jnp.int32)
    xp, xs = x_prompt, x_sample
    rows_p, rows_s = [], []
    for l in range(DEPTH):
        lp = (w_ada[l], b_ada[l], norm_ffn1[l], w1_ffn1[l], w3_ffn1[l], w2_ffn1[l],
              norm_attn[l], w_in[l], qnorm_a[l], knorm_a[l], knorm_idx[l], qnorm_b[l], knorm_b[l],
              lam_q1[l], lam_k1[l], lam_q2[l], lam_k2[l], subln_b[l], w_out[l],
              norm_ffn2[l], w1_ffn2[l], w3_ffn2[l], w2_ffn2[l])
        xp, rp = layer(xp, c_prompt, pos_p, None, lp, l)
        past = (cache_k_a[l], cache_v_a[l], cache_k_idx[l], cache_k_b[l], cache_v_b[l])
        xs, rs = layer(xs, c_sample, pos_s, past, lp, l)
        rows_p.append(rp)
        rows_s.append(rs)
    k_a_p = jnp.stack([r[0] for r in rows_p])
    v_a_p = jnp.stack([r[1] for r in rows_p])
    k_idx_p = jnp.stack([r[2] for r in rows_p])
    k_b_p = jnp.stack([r[3] for r in rows_p])
    v_b_p = jnp.stack([r[4] for r in rows_p])
    k_a_s = jnp.stack([r[0] for r in rows_s])
    v_a_s = jnp.stack([r[1] for r in rows_s])
    k_idx_s = jnp.stack([r[2] for r in rows_s])
    k_b_s = jnp.stack([r[3] for r in rows_s])
    v_b_s = jnp.stack([r[4] for r in rows_s])
    return (xp, xs, k_a_p, v_a_p, k_idx_p, k_b_p, v_b_p, k_a_s, v_a_s, k_idx_s, k_b_s, v_b_s)
```

```python
import functools
import math

import numpy as np
import jax
import jax.numpy as jnp
from jax import lax
from jax.experimental import pallas as pl
from jax.experimental.pallas import tpu as pltpu

F32 = jnp.float32
BF16 = jnp.bfloat16

CHUNK = 64
N_IDX_HEADS = 16
TOPK_MAX = 256
ROPE_THETA = 500000.0
ROT_FRAC = 4
EPS = 1e-6
N_MOD = 9

LANES = 128
V7X_VMEM_BYTES = 64 * 1024 * 1024
VMEM_LIMIT = V7X_VMEM_BYTES * 7 // 8
NEG = -1e30
NT_DIMS = (((1,), (1,)), ((), ()))


def _lambda_init(layer_idx):
    return 0.8 - 0.6 * math.exp(-0.3 * layer_idx)


def _pick(n, candidates):
    for c in candidates:
        if n % c == 0:
            return c
    return n


def _cparams(semantics):
    return pltpu.CompilerParams(dimension_semantics=semantics, vmem_limit_bytes=VMEM_LIMIT)


def _round_up(n, m):
    return (n + m - 1) // m * m


def _ada_kernel(c_ref, w_ref, b_ref, o_ref):
    c = c_ref[...]
    s = (c * jax.nn.sigmoid(c)).astype(BF16)
    o_ref[...] = jnp.dot(s, w_ref[...].astype(BF16), preferred_element_type=F32) + b_ref[...]


def _ada(c, w, b):
    R, D = c.shape
    N = w.shape[1]
    tn = _pick(N, (512, 256, 128))
    return pl.pallas_call(
        _ada_kernel,
        grid=(N // tn,),
        in_specs=[pl.BlockSpec((R, D), lambda j: (0, 0)),
                  pl.BlockSpec((D, tn), lambda j: (0, j)),
                  pl.BlockSpec((1, tn), lambda j: (0, j))],
        out_specs=pl.BlockSpec((R, tn), lambda j: (0, j)),
        out_shape=jax.ShapeDtypeStruct((R, N), F32),
        compiler_params=_cparams(("parallel",)),
        name="ada_matvec",
    )(c, w, b)


def _norm_mod_kernel(x_ref, g_ref, sc_ref, sh_ref, o_ref):
    x = x_ref[...]
    ms = jnp.mean(x * x, axis=-1, keepdims=True)
    y = x * lax.rsqrt(ms + EPS) * g_ref[...]
    o_ref[...] = (y * (1.0 + sc_ref[...]) + sh_ref[...]).astype(o_ref.dtype)


def _norm_mod(x, g, scale, shift):
    M, D = x.shape
    tm = _pick(M, (256, 128, 64, 32, 16, 8))

    def mspec(a):
        if a.shape[0] == 1:
            return pl.BlockSpec((1, D), lambda i: (0, 0))
        return pl.BlockSpec((tm, D), lambda i: (i, 0))

    return pl.pallas_call(
        _norm_mod_kernel,
        grid=(M // tm,),
        in_specs=[pl.BlockSpec((tm, D), lambda i: (i, 0)),
                  pl.BlockSpec((1, D), lambda i: (0, 0)),
                  mspec(scale), mspec(shift)],
        out_specs=pl.BlockSpec((tm, D), lambda i: (i, 0)),
        out_shape=jax.ShapeDtypeStruct((M, D), BF16),
        compiler_params=_cparams(("parallel",)),
        name="norm_mod",
    )(x, g, scale, shift)


def _mm_up_kernel(h_ref, w1_ref, w3_ref, o_ref):
    h = h_ref[...]
    u = jnp.dot(h, w1_ref[...], preferred_element_type=F32)
    v = jnp.dot(h, w3_ref[...], preferred_element_type=F32)
    o_ref[...] = (u * jax.nn.sigmoid(u) * v).astype(o_ref.dtype)


def _mm_up(h, w1, w3):
    M, K = h.shape
    F = w1.shape[1]
    tm = _pick(M, (1024, 512, 256, 128, 64, 32, 16, 8))
    tn = _pick(F, (256, 128))
    return pl.pallas_call(
        _mm_up_kernel,
        grid=(M // tm, F // tn),
        in_specs=[pl.BlockSpec((tm, K), lambda i, j: (i, 0)),
                  pl.BlockSpec((K, tn), lambda i, j: (0, j)),
                  pl.BlockSpec((K, tn), lambda i, j: (0, j))],
        out_specs=pl.BlockSpec((tm, tn), lambda i, j: (i, j)),
        out_shape=jax.ShapeDtypeStruct((M, F), BF16),
        compiler_params=_cparams(("parallel", "parallel")),
        name="ffn_up",
    )(h, w1, w3)


def _mm_plain_kernel(h_ref, w_ref, o_ref):
    o_ref[...] = jnp.dot(h_ref[...], w_ref[...], preferred_element_type=F32)


def _mm_plain(h, w):
    M, K = h.shape
    N = w.shape[1]
    tm = _pick(M, (1024, 512, 256, 128, 64, 32, 16, 8))
    tn = _pick(N, (512, 256, 128))
    return pl.pallas_call(
        _mm_plain_kernel,
        grid=(M // tm, N // tn),
        in_specs=[pl.BlockSpec((tm, K), lambda i, j: (i, 0)),
                  pl.BlockSpec((K, tn), lambda i, j: (0, j))],
        out_specs=pl.BlockSpec((tm, tn), lambda i, j: (i, j)),
        out_shape=jax.ShapeDtypeStruct((M, N), F32),
        compiler_params=_cparams(("parallel", "parallel")),
        name="in_proj",
    )(h, w)


def _mm_res_kernel(*refs, n_lhs, coef):
    a_refs = refs[:n_lhs]
    w_refs = refs[n_lhs:2 * n_lhs]
    x_ref, g_ref, o_ref = refs[2 * n_lhs:]
    y = jnp.dot(a_refs[0][...], w_refs[0][...], preferred_element_type=F32)
    for a_ref, w_ref in zip(a_refs[1:], w_refs[1:]):
        y = y + jnp.dot(a_ref[...], w_ref[...], preferred_element_type=F32)
    o_ref[...] = x_ref[...] + (coef * g_ref[...]) * y


def _mm_res(a_list, w_list, x, gate, coef, name):
    M, N = x.shape
    ktot = sum(a.shape[1] for a in a_list)
    tm = _pick(M, (1024, 512, 256, 128, 64, 32, 16, 8)) if ktot <= 4096 else \
        _pick(M, (512, 256, 128, 64, 32, 16, 8))
    tn = _pick(N, (512, 256, 128)) if ktot <= 4096 else _pick(N, (256, 128))
    in_specs = [pl.BlockSpec((tm, a.shape[1]), lambda i, j: (i, 0)) for a in a_list]
    in_specs += [pl.BlockSpec((w.shape[0], tn), lambda i, j: (0, j)) for w in w_list]
    in_specs.append(pl.BlockSpec((tm, tn), lambda i, j: (i, j)))
    if gate.shape[0] == 1:
        in_specs.append(pl.BlockSpec((1, tn), lambda i, j: (0, j)))
    else:
        in_specs.append(pl.BlockSpec((tm, tn), lambda i, j: (i, j)))
    return pl.pallas_call(
        functools.partial(_mm_res_kernel, n_lhs=len(a_list), coef=coef),
        grid=(M // tm, N // tn),
        in_specs=in_specs,
        out_specs=pl.BlockSpec((tm, tn), lambda i, j: (i, j)),
        out_shape=jax.ShapeDtypeStruct((M, N), F32),
        compiler_params=_cparams(("parallel", "parallel")),
        name=name,
    )(*a_list, *w_list, x, gate)


class _Layout:
    def __init__(self, ha, kv, hb, hd, di):
        self.ha, self.kv, self.hb, self.hd, self.di = ha, kv, hb, hd, di
        self.q_a = 0
        self.k_a = self.q_a + ha * hd
        self.v_a = self.k_a + kv * hd
        self.q_i = self.v_a + kv * hd
        self.kiw = self.q_i + N_IDX_HEADS * di
        self.q_b = self.kiw + LANES
        self.k_b = self.q_b + hb * 2 * hd
        self.v_b = self.k_b + hb * 2 * hd
        self.end = self.v_b + hb * 2 * hd
        self.width = _round_up(self.end, 512)


def _rope_tables(hd, di):
    def table(d):
        r = d // ROT_FRAC
        half = r // 2
        inv = ROPE_THETA ** (-(jnp.arange(half, dtype=jnp.float32) * 2.0 / r))
        lane = np.arange(LANES) % d
        idx = np.where(lane < half, lane, np.where(lane < r, lane - half, 0))
        invl = jnp.where(lane < r, inv[idx], 0.0)
        m_hi = ((lane >= half) & (lane < r)).astype(np.float32)
        m_lo = -(lane < half).astype(np.float32)
        return jnp.stack([invl, jnp.asarray(m_hi), jnp.asarray(m_lo)])[:, None, :], half
    th, half_h = table(hd)
    ti, half_i = table(di)
    return th, half_h, ti, half_i


def _post_kernel(z_ref, gqa_ref, gka_ref, gki_ref, gqb_ref, gkb_ref, th_ref, ti_ref,
                 ka_ref, va_ref, ki_ref, kb_ref, vb_ref,
                 qa16_ref, ka16_ref, va16_ref, qi16_ref, kiw_ref, kiw16_ref, qb16_ref, kb16_ref, vb16_ref,
                 *, lay, ts, pos_off, half_h, half_i, scale):
    hd, di = lay.hd, lay.di
    row = lax.broadcasted_iota(jnp.int32, (ts, LANES), 0)
    pos = (pos_off + pl.program_id(1) * ts + row).astype(F32)

    def rope_fn(t_ref, half):
        ang = pos * t_ref[0]
        cosv = jnp.cos(ang)
        sinv = jnp.sin(ang)
        s_hi = sinv * t_ref[1]
        s_lo = sinv * t_ref[2]

        def rope(x):
            return x * cosv + pltpu.roll(x, half, 1) * s_hi + pltpu.roll(x, LANES - half, 1) * s_lo
        return rope

    rope_h = rope_fn(th_ref, half_h)
    rope_i = rope_fn(ti_ref, half_i)

    def hnorm(x, g):
        ms = jnp.mean(x * x, axis=-1, keepdims=True)
        return x * lax.rsqrt(ms + EPS) * g

    def slab(off, h):
        return z_ref[0, :, off + h * hd: off + (h + 1) * hd]

    def cols(h):
        return slice(h * hd, (h + 1) * hd)

    for h in range(lay.ha):
        q = rope_h(hnorm(slab(lay.q_a, h), gqa_ref[...]))
        qa16_ref[0, :, cols(h)] = (q * scale).astype(BF16)
    for h in range(lay.kv):
        k = rope_h(hnorm(slab(lay.k_a, h), gka_ref[...]))
        ka_ref[0, :, cols(h)] = k
        ka16_ref[0, :, cols(h)] = k.astype(BF16)
        v = slab(lay.v_a, h)
        va_ref[0, :, cols(h)] = v
        va16_ref[0, :, cols(h)] = v.astype(BF16)
    for h in range(N_IDX_HEADS * di // LANES):
        qi16_ref[0, :, cols(h)] = rope_i(slab(lay.q_i, h)).astype(BF16)
    x = slab(lay.kiw, 0)
    lane = lax.broadcasted_iota(jnp.int32, (ts, LANES), 1)
    is_k = lane < di
    ms = jnp.sum(jnp.where(is_k, x * x, 0.0), axis=-1, keepdims=True) * (1.0 / di)
    ki = rope_i(x * lax.rsqrt(ms + EPS) * gki_ref[...])
    ki_ref[0] = ki[:, :di]
    kiw = jnp.where(is_k, ki, x)
    kiw_ref[0] = kiw
    kiw16_ref[0] = kiw.astype(BF16)
    for h in range(2 * lay.hb):
        q = rope_h(hnorm(slab(lay.q_b, h), gqb_ref[...]))
        qb16_ref[0, :, cols(h)] = (q * scale).astype(BF16)
        k = rope_h(hnorm(slab(lay.k_b, h), gkb_ref[...]))
        kb_ref[0, :, cols(h)] = k
        kb16_ref[0, :, cols(h)] = k.astype(BF16)
        v = slab(lay.v_b, h)
        vb_ref[0, :, cols(h)] = v
        vb16_ref[0, :, cols(h)] = v.astype(BF16)


def _post(z, lay, gains, tables, pos_off):
    B, S, W = z.shape
    hd, di = lay.hd, lay.di
    ts = _pick(S, (128, 64, 32, 16, 8))
    th, half_h, ti, half_i = tables
    wa, wkv, wqi, wb = lay.ha * hd, lay.kv * hd, N_IDX_HEADS * di, lay.hb * 2 * hd

    def rows(width):
        return pl.BlockSpec((1, ts, width), lambda b, s: (b, s, 0))

    def const(a):
        return pl.BlockSpec(a.shape, lambda b, s: (0,) * a.ndim)

    out_widths = [(wkv, F32), (wkv, F32), (di, F32), (wb, F32), (wb, F32),
                  (wa, BF16), (wkv, BF16), (wkv, BF16), (wqi, BF16), (LANES, F32), (LANES, BF16),
                  (wb, BF16), (wb, BF16), (wb, BF16)]
    return pl.pallas_call(
        functools.partial(_post_kernel, lay=lay, ts=ts, pos_off=pos_off, half_h=half_h, half_i=half_i,
                          scale=hd ** -0.5),
        grid=(B, S // ts),
        in_specs=[rows(W)] + [const(g) for g in gains] + [const(th), const(ti)],
        out_specs=[rows(w) for w, _ in out_widths],
        out_shape=[jax.ShapeDtypeStruct((B, S, w), dt) for w, dt in out_widths],
        compiler_params=_cparams(("parallel", "parallel")),
        name="head_norm_rope",
    )(z, *gains, th, ti)


def _tile_bounds(p0, tq, tk, n_keys):
    a_full = jnp.minimum((p0 // CHUNK + 1) * CHUNK, n_keys)
    a_tot = jnp.minimum(((p0 + tq - 1) // CHUNK + 1) * CHUNK, n_keys)
    return a_full // tk, (a_tot + tk - 1) // tk


def _allowed(j, p0, tq, tk, n_keys):
    shift = CHUNK.bit_length() - 1
    kpos = j * tk + lax.broadcasted_iota(jnp.int32, (tq, tk), 1)
    qpos = p0 + lax.broadcasted_iota(jnp.int32, (tq, tk), 0)
    return (lax.shift_right_logical(kpos, shift) <= lax.shift_right_logical(qpos, shift)) & (kpos < n_keys)


def _lane_tile(x, reps):
    return x if reps == 1 else jnp.concatenate([x] * reps, axis=1)


def _softmax_step(s, v, m_ref, l_ref, acc_ref, idx):
    tk = s.shape[1]
    wv = acc_ref.shape[-1]
    m_old = m_ref[idx]
    m_new = jnp.maximum(m_old, jnp.max(s, axis=1, keepdims=True))
    alpha = jnp.exp(m_old - m_new)
    p = jnp.exp(s - _lane_tile(m_new, tk // LANES))
    l_ref[idx] = alpha * l_ref[idx] + jnp.sum(p, axis=1, keepdims=True)
    acc_ref[idx] = _lane_tile(alpha, wv // LANES) * acc_ref[idx] + jnp.dot(
        p.astype(BF16), v, preferred_element_type=F32)
    m_ref[idx] = m_new


def _dsa_kernel(qa_ref, qi_ref, wq_ref, ka_ref, va_ref, ki_ref, o_ref,
                keys_ref, wexp_ref, qih_ref, tau_ref, m_ref, l_ref, acc_ref,
                *, tq, tk, n_keys, pos_off, topk, ha, kv, hd, di):
    p0 = pos_off + pl.program_id(1) * tq
    n_full, n_tot = _tile_bounds(p0, tq, tk, n_keys)
    reps = tk // LANES
    group = ha // kv
    int_min = jnp.int32(-2 ** 31)

    lane = lax.broadcasted_iota(jnp.int32, (tq, LANES), 1)
    w_scale = (N_IDX_HEADS ** -0.5) * (di ** -0.5)
    for h in range(N_IDX_HEADS):
        pair = qi_ref[0, :, (h * di // LANES) * LANES:(h * di // LANES + 1) * LANES].astype(F32)
        off = (h * di) % LANES
        if off:
            pair = pltpu.roll(pair, LANES - off, 1)
        qih_ref[h] = jnp.where(lane < di, pair, 0.0).astype(BF16)
        wexp_ref[h] = jnp.broadcast_to(wq_ref[0, :, di + h:di + h + 1] * w_scale, (tq, tk))

    def key_slice(j):
        return pl.ds(pl.multiple_of(j * tk, tk), tk)

    def score_tile(j, masked):
        kt = ki_ref[0, key_slice(j), :]
        sc = jnp.zeros((tq, tk), F32)
        for h in range(N_IDX_HEADS):
            lg = lax.dot_general(qih_ref[h], kt, NT_DIMS, preferred_element_type=F32)
            sc = sc + jnp.maximum(lg, 0.0) * wexp_ref[h]
        if masked:
            sc = jnp.where(_allowed(j, p0, tq, tk, n_keys), sc, -jnp.inf)
        bits = pltpu.bitcast(sc, jnp.int32)
        keys_ref[j] = jnp.where(bits < 0, bits ^ jnp.int32(0x7FFFFFFF), bits)

    def score_full(j, c):
        score_tile(j, False)
        return c

    def score_masked(j, c):
        score_tile(j, True)
        return c

    lax.fori_loop(0, n_full, score_full, 0)
    lax.fori_loop(n_full, n_tot, score_masked, 0)

    def count_ge(cand):
        def body(j, acc):
            t = keys_ref[j]
            for c in range(reps):
                acc = acc + jnp.where(t[:, c * LANES:(c + 1) * LANES] >= cand, 1.0, 0.0)
            return acc
        acc = lax.fori_loop(0, n_tot, body, jnp.zeros((tq, LANES), F32))
        return jnp.sum(acc, axis=1, keepdims=True)

    kf = float(topk)
    zero = jnp.zeros((tq, LANES), jnp.int32)
    prefix = jnp.where(count_ge(zero) >= kf, zero, int_min)

    def bit_body(b, prefix):
        cand = prefix + lax.shift_left(jnp.int32(1), 30 - b)
        return jnp.where(count_ge(cand) >= kf, cand, prefix)

    prefix = lax.fori_loop(0, 31, bit_body, prefix)
    tau_ref[...] = _lane_tile(prefix, reps)

    m_ref[...] = jnp.full(m_ref.shape, NEG, F32)
    l_ref[...] = jnp.zeros(l_ref.shape, F32)
    acc_ref[...] = jnp.zeros(acc_ref.shape, F32)

    def attn_tile(j, masked):
        sel = keys_ref[j] >= tau_ref[...]
        if masked:
            sel = sel & _allowed(j, p0, tq, tk, n_keys)
        for h in range(ha):
            g = h // group
            k = ka_ref[0, key_slice(j), g * hd:(g + 1) * hd]
            v = va_ref[0, key_slice(j), g * hd:(g + 1) * hd]
            s = lax.dot_general(qa_ref[0, :, h * hd:(h + 1) * hd], k, NT_DIMS, preferred_element_type=F32)
            _softmax_step(jnp.where(sel, s, NEG), v, m_ref, l_ref, acc_ref, h)

    def attn_full(j, c):
        attn_tile(j, False)
        return c

    def attn_masked(j, c):
        attn_tile(j, True)
        return c

    lax.fori_loop(0, n_full, attn_full, 0)
    lax.fori_loop(n_full, n_tot, attn_masked, 0)
    for h in range(ha):
        o_ref[0, :, h * hd:(h + 1) * hd] = (acc_ref[h] / l_ref[h]).astype(o_ref.dtype)


def _dsa(qa, qi, kiw_q, ka, va, kiw_k, n_keys, pos_off, kv, hd, di):
    B, S, wa = qa.shape
    lp = ka.shape[1]
    ha = wa // hd
    tq = _pick(S, (128, 64, 32, 16, 8))
    tk = 256
    topk = min(TOPK_MAX, n_keys // 4)
    kern = functools.partial(_dsa_kernel, tq=tq, tk=tk, n_keys=n_keys, pos_off=pos_off, topk=topk,
                             ha=ha, kv=kv, hd=hd, di=di)

    def rows(width):
        return pl.BlockSpec((1, tq, width), lambda b, i: (b, i, 0))

    def resident(width):
        return pl.BlockSpec((1, lp, width), lambda b, i: (b, 0, 0), pipeline_mode=pl.Buffered(1))

    return pl.pallas_call(
        kern,
        grid=(B, S // tq),
        in_specs=[rows(wa), rows(qi.shape[2]), rows(LANES),
                  resident(kv * hd), resident(kv * hd), resident(LANES)],
        out_specs=rows(wa),
        out_shape=jax.ShapeDtypeStruct((B, S, wa), BF16),
        scratch_shapes=[pltpu.VMEM((lp // tk, tq, tk), jnp.int32),
                        pltpu.VMEM((N_IDX_HEADS, tq, tk), F32),
                        pltpu.VMEM((N_IDX_HEADS, tq, LANES), BF16),
                        pltpu.VMEM((tq, tk), jnp.int32),
                        pltpu.VMEM((ha, tq, LANES), F32),
                        pltpu.VMEM((ha, tq, LANES), F32),
                        pltpu.VMEM((ha, tq, hd), F32)],
        compiler_params=_cparams(("parallel", "arbitrary")),
        name="dsa_attention",
    )(qa, qi, kiw_q, ka, va, kiw_k)


def _diff_kernel(q_ref, k_ref, v_ref, lq1_ref, lk1_ref, lq2_ref, lk2_ref, sg_ref, o_ref,
                 m_ref, l_ref, acc_ref, *, tq, tk, n_keys, pos_off, hd, lam_init):
    p0 = pos_off + pl.program_id(2) * tq
    n_full, n_tot = _tile_bounds(p0, tq, tk, n_keys)
    db = 2 * hd
    m_ref[...] = jnp.full(m_ref.shape, NEG, F32)
    l_ref[...] = jnp.zeros(l_ref.shape, F32)
    acc_ref[...] = jnp.zeros(acc_ref.shape, F32)

    def tile(j, masked):
        ks = pl.ds(pl.multiple_of(j * tk, tk), tk)
        v = v_ref[0, ks, :]
        ok = _allowed(j, p0, tq, tk, n_keys) if masked else None
        for c in range(2):
            s = lax.dot_general(q_ref[0, :, c * hd:(c + 1) * hd], k_ref[0, ks, c * hd:(c + 1) * hd],
                                NT_DIMS, preferred_element_type=F32)
            if masked:
                s = jnp.where(ok, s, NEG)
            _softmax_step(s, v, m_ref, l_ref, acc_ref, c)

    def tile_full(j, c):
        tile(j, False)
        return c

    def tile_masked(j, c):
        tile(j, True)
        return c

    lax.fori_loop(0, n_full, tile_full, 0)
    lax.fori_loop(n_full, n_tot, tile_masked, 0)

    lam = (jnp.exp(jnp.sum(lq1_ref[...] * lk1_ref[...], axis=-1, keepdims=True))
           - jnp.exp(jnp.sum(lq2_ref[...] * lk2_ref[...], axis=-1, keepdims=True)) + lam_init)
    reps = db // LANES
    o = acc_ref[0] / _lane_tile(l_ref[0], reps) - lam * (acc_ref[1] / _lane_tile(l_ref[1], reps))
    ms = jnp.mean(o * o, axis=-1, keepdims=True)
    o_ref[0] = (o * lax.rsqrt(ms + EPS) * sg_ref[...] * (1.0 - lam_init)).astype(o_ref.dtype)


def _diff(qb, kb, vb, lams, subln_g, n_keys, pos_off, hd, lam_init):
    B, S, wb = qb.shape
    lp = kb.shape[1]
    db = 2 * hd
    hb = wb // db
    tq = _pick(S, (256, 128, 64, 32, 16, 8))
    tk = 256
    kern = functools.partial(_diff_kernel, tq=tq, tk=tk, n_keys=n_keys, pos_off=pos_off, hd=hd,
                             lam_init=lam_init)
    vec = pl.BlockSpec((1, hd), lambda b, h, i: (0, 0))
    return pl.pallas_call(
        kern,
        grid=(B, hb, S // tq),
        in_specs=[pl.BlockSpec((1, tq, db), lambda b, h, i: (b, i, h)),
                  pl.BlockSpec((1, lp, db), lambda b, h, i: (b, 0, h)),
                  pl.BlockSpec((1, lp, db), lambda b, h, i: (b, 0, h)),
                  vec, vec, vec, vec,
                  pl.BlockSpec((1, db), lambda b, h, i: (0, 0))],
        out_specs=pl.BlockSpec((1, tq, db), lambda b, h, i: (b, i, h)),
        out_shape=jax.ShapeDtypeStruct((B, S, wb), BF16),
        scratch_shapes=[pltpu.VMEM((2, tq, LANES), F32),
                        pltpu.VMEM((2, tq, LANES), F32),
                        pltpu.VMEM((2, tq, db), F32)],
        compiler_params=_cparams(("parallel", "parallel", "arbitrary")),
        name="diff_attention",
    )(qb, kb, vb, *lams, subln_g)


def _pad_rows(a, lp):
    return jnp.pad(a, ((0, 0), (0, lp - a.shape[1]), (0, 0)))


def _prep_w_in(w_in, lay, d_idx_w):
    ha, kv, hb, hd, di = lay.ha, lay.kv, lay.hb, lay.hd, lay.di
    widths = (ha * hd, kv * hd, kv * hd, N_IDX_HEADS * di, di, d_idx_w, hb * 2 * hd, hb * 2 * hd, hb * 2 * hd)
    offs = np.cumsum((0,) + widths)
    seg = [w_in[:, offs[i]:offs[i + 1]] for i in range(len(widths))]
    d = w_in.shape[0]
    pad_kiw = jnp.zeros((d, LANES - di - d_idx_w), w_in.dtype)
    pad_end = jnp.zeros((d, lay.width - lay.end), w_in.dtype)
    return jnp.concatenate(seg[:6] + [pad_kiw] + seg[6:] + [pad_end], axis=1).astype(BF16)


def _layer(x, B, S, mods, past, pos_off, lw, li):
    (sh1, sc1, gt1, sh2, sc2, gt2, sh3, sc3, gt3) = mods
    D = x.shape[1]
    lay = lw["lay"]
    hd, di, kv = lay.hd, lay.di, lay.kv

    h = _norm_mod(x, lw["norm_ffn1"], sc1, sh1)
    a = _mm_up(h, lw["w1_1"], lw["w3_1"])
    x = _mm_res([a], [lw["w2_1"]], x, gt1, 0.5, "ffn_down")

    h = _norm_mod(x, lw["norm_attn"], sc2, sh2)
    z = _mm_plain(h, lw["w_in"]).reshape(B, S, lay.width)
    (k_a, v_a, k_i, k_b, v_b, qa16, ka16, va16, qi16, kiw, kiw16, qb16, kb16, vb16) = _post(
        z, lay, lw["gains"], lw["tables"], pos_off)

    if past is not None:
        ck_a, cv_a, ck_i, ck_b, cv_b = past
        P = ck_a.shape[1]
        flat = lambda c: c.reshape(B, P, -1).astype(BF16)
        ka16 = jnp.concatenate([flat(ck_a), ka16], axis=1)
        va16 = jnp.concatenate([flat(cv_a), va16], axis=1)
        ck_i16 = jnp.pad(flat(ck_i), ((0, 0), (0, 0), (0, LANES - di)))
        kiw16_k = jnp.concatenate([ck_i16, kiw16], axis=1)
        kb16 = jnp.concatenate([flat(ck_b), kb16], axis=1)
        vb16 = jnp.concatenate([flat(cv_b), vb16], axis=1)
        n_keys = P + S
    else:
        kiw16_k = kiw16
        n_keys = S
    lp = _round_up(n_keys, 256)
    if lp != n_keys:
        ka16, va16, kiw16_k, kb16, vb16 = (_pad_rows(t, lp) for t in (ka16, va16, kiw16_k, kb16, vb16))

    oa = _dsa(qa16, qi16, kiw, ka16, va16, kiw16_k, n_keys, pos_off, kv, hd, di)
    ob = _diff(qb16, kb16, vb16, lw["lams"], lw["subln_g"], n_keys, pos_off, hd, _lambda_init(li))
    wa = oa.shape[2]
    x = _mm_res([oa.reshape(B * S, wa), ob.reshape(B * S, -1)], [lw["w_out"][:wa], lw["w_out"][wa:]],
                x, gt2, 1.0, "out_proj")

    h = _norm_mod(x, lw["norm_ffn2"], sc3, sh3)
    a = _mm_up(h, lw["w1_2"], lw["w3_2"])
    x = _mm_res([a], [lw["w2_2"]], x, gt3, 0.5, "ffn_down")
    return x, (k_a, v_a, k_i, k_b, v_b)


def kernel(x_prompt, x_sample, cache_k_a, cache_v_a, cache_k_idx, cache_k_b, cache_v_b, c_prompt, c_sample, w_ada, b_ada, norm_ffn1, w1_ffn1, w3_ffn1, w2_ffn1, norm_attn, w_in, qnorm_a, knorm_a, knorm_idx, qnorm_b, knorm_b, lam_q1, lam_k1, lam_q2, lam_k2, subln_b, w_out, norm_ffn2, w1_ffn2, w3_ffn2, w2_ffn2):
    depth = w_ada.shape[0]
    Bp, Sp, D = x_prompt.shape
    Bs, Ss, _ = x_sample.shape
    P = cache_k_a.shape[2]
    kv, hd = cache_k_a.shape[3], cache_k_a.shape[4]
    di = cache_k_idx.shape[3]
    hb = cache_k_b.shape[3]
    db = cache_v_b.shape[4]
    ha = (w_out.shape[1] - hb * db) // hd
    lay = _Layout(ha, kv, hb, hd, di)
    tables = _rope_tables(hd, di)

    xp = x_prompt.reshape(Bp * Sp, D)
    xs = x_sample.reshape(Bs * Ss, D)
    n_c = Bp + Bs
    c_all = jnp.pad(jnp.concatenate([c_prompt, c_sample], axis=0), ((0, _round_up(n_c, 8) - n_c), (0, 0)))

    rows_p, rows_s = [], []
    for l in range(depth):
        row = lambda a: a[l][None, :]
        gki = jnp.pad(knorm_idx[l], (0, LANES - di))[None, :]
        lw = dict(
            lay=lay, tables=tables,
            norm_ffn1=row(norm_ffn1), norm_attn=row(norm_attn), norm_ffn2=row(norm_ffn2),
            w1_1=w1_ffn1[l].astype(BF16), w3_1=w3_ffn1[l].astype(BF16), w2_1=w2_ffn1[l].astype(BF16),
            w1_2=w1_ffn2[l].astype(BF16), w3_2=w3_ffn2[l].astype(BF16), w2_2=w2_ffn2[l].astype(BF16),
            w_in=_prep_w_in(w_in[l], lay, N_IDX_HEADS), w_out=w_out[l].astype(BF16),
            gains=(row(qnorm_a), row(knorm_a), gki, row(qnorm_b), row(knorm_b)),
            lams=(row(lam_q1), row(lam_k1), row(lam_q2), row(lam_k2)),
            subln_g=row(subln_b),
        )
        mod = _ada(c_all, w_ada[l], b_ada[l][None, :])

        def mods_for(lo, nb, s):
            m = mod[lo:lo + nb].reshape(nb, N_MOD, D)
            if nb == 1:
                return tuple(m[:, i] for i in range(N_MOD))
            return tuple(jnp.repeat(m[:, i], s, axis=0) for i in range(N_MOD))

        xp, rp = _layer(xp, Bp, Sp, mods_for(0, Bp, Sp), None, 0, lw, l)
        past = (cache_k_a[l], cache_v_a[l], cache_k_idx[l], cache_k_b[l], cache_v_b[l])
        xs, rs = _layer(xs, Bs, Ss, mods_for(Bp, Bs, Ss), past, P, lw, l)
        rows_p.append(rp)
        rows_s.append(rs)

    def stack(rows, idx, B, S, tail):
        return jnp.stack([r[idx].reshape((B, S) + tail) for r in rows])

    tails = ((kv, hd), (kv, hd), (di,), (hb, 2, hd), (hb, db))
    outs_p = tuple(stack(rows_p, i, Bp, Sp, tails[i]) for i in range(5))
    outs_s = tuple(stack(rows_s, i, Bs, Ss, tails[i]) for i in range(5))
    return (xp.reshape(Bp, Sp, D), xs.reshape(Bs, Ss, D)) + outs_p + outs_s
```

```python
import functools
import math

import numpy as np
import jax
import jax.numpy as jnp
from jax import lax
from jax.experimental import pallas as pl
from jax.experimental.pallas import tpu as pltpu

F32 = jnp.float32
BF16 = jnp.bfloat16

CHUNK = 64
N_IDX_HEADS = 16
TOPK_MAX = 256
ROPE_THETA = 500000.0
ROT_FRAC = 4
EPS = 1e-6
N_MOD = 9

LANES = 128
V7X_VMEM_BYTES = 64 * 1024 * 1024
VMEM_LIMIT = V7X_VMEM_BYTES * 7 // 8
NEG = -1e30
NT_DIMS = (((1,), (1,)), ((), ()))


def _lambda_init(layer_idx):
    return 0.8 - 0.6 * math.exp(-0.3 * layer_idx)


def _pick(n, candidates):
    for c in candidates:
        if n % c == 0:
            return c
    return n


def _cparams(semantics):
    return pltpu.CompilerParams(dimension_semantics=semantics, vmem_limit_bytes=VMEM_LIMIT)


def _round_up(n, m):
    return (n + m - 1) // m * m


def _ada_kernel(c_ref, w_ref, b_ref, o_ref):
    c = c_ref[...]
    s = (c * jax.nn.sigmoid(c)).astype(BF16)
    o_ref[...] = jnp.dot(s, w_ref[...].astype(BF16), preferred_element_type=F32) + b_ref[...]


def _ada(c, w, b):
    R, D = c.shape
    N = w.shape[1]
    tn = _pick(N, (512, 256, 128))
    return pl.pallas_call(
        _ada_kernel,
        grid=(N // tn,),
        in_specs=[pl.BlockSpec((R, D), lambda j: (0, 0)),
                  pl.BlockSpec((D, tn), lambda j: (0, j)),
                  pl.BlockSpec((1, tn), lambda j: (0, j))],
        out_specs=pl.BlockSpec((R, tn), lambda j: (0, j)),
        out_shape=jax.ShapeDtypeStruct((R, N), F32),
        compiler_params=_cparams(("parallel",)),
        name="ada_matvec",
    )(c, w, b)


def _norm_mod_kernel(x_ref, g_ref, sc_ref, sh_ref, o_ref):
    x = x_ref[...]
    ms = jnp.mean(x * x, axis=-1, keepdims=True)
    y = x * lax.rsqrt(ms + EPS) * g_ref[...]
    o_ref[...] = (y * (1.0 + sc_ref[...]) + sh_ref[...]).astype(o_ref.dtype)


def _norm_mod(x, g, scale, shift):
    M, D = x.shape
    tm = _pick(M, (256, 128, 64, 32, 16, 8))

    def mspec(a):
        if a.shape[0] == 1:
            return pl.BlockSpec((1, D), lambda i: (0, 0))
        return pl.BlockSpec((tm, D), lambda i: (i, 0))

    return pl.pallas_call(
        _norm_mod_kernel,
        grid=(M // tm,),
        in_specs=[pl.BlockSpec((tm, D), lambda i: (i, 0)),
                  pl.BlockSpec((1, D), lambda i: (0, 0)),
                  mspec(scale), mspec(shift)],
        out_specs=pl.BlockSpec((tm, D), lambda i: (i, 0)),
        out_shape=jax.ShapeDtypeStruct((M, D), BF16),
        compiler_params=_cparams(("parallel",)),
        name="norm_mod",
    )(x, g, scale, shift)


def _mm_up_kernel(h_ref, w1_ref, w3_ref, o_ref):
    h = h_ref[...]
    u = jnp.dot(h, w1_ref[...], preferred_element_type=F32)
    v = jnp.dot(h, w3_ref[...], preferred_element_type=F32)
    o_ref[...] = (u * jax.nn.sigmoid(u) * v).astype(o_ref.dtype)


def _mm_up(h, w1, w3):
    M, K = h.shape
    F = w1.shape[1]
    tm = _pick(M, (1024, 512, 256, 128, 64, 32, 16, 8))
    tn = _pick(F, (256, 128))
    return pl.pallas_call(
        _mm_up_kernel,
        grid=(M // tm, F // tn),
        in_specs=[pl.BlockSpec((tm, K), lambda i, j: (i, 0)),
                  pl.BlockSpec((K, tn), lambda i, j: (0, j)),
                  pl.BlockSpec((K, tn), lambda i, j: (0, j))],
        out_specs=pl.BlockSpec((tm, tn), lambda i, j: (i, j)),
        out_shape=jax.ShapeDtypeStruct((M, F), BF16),
        compiler_params=_cparams(("parallel", "parallel")),
        name="ffn_up",
    )(h, w1, w3)


def _mm_plain_kernel(h_ref, w_ref, o_ref):
    o_ref[...] = jnp.dot(h_ref[...], w_ref[...], preferred_element_type=F32)


def _mm_plain(h, w):
    M, K = h.shape
    N = w.shape[1]
    tm = _pick(M, (1024, 512, 256, 128, 64, 32, 16, 8))
    tn = _pick(N, (512, 256, 128))
    return pl.pallas_call(
        _mm_plain_kernel,
        grid=(M // tm, N // tn),
        in_specs=[pl.BlockSpec((tm, K), lambda i, j: (i, 0)),
                  pl.BlockSpec((K, tn), lambda i, j: (0, j))],
        out_specs=pl.BlockSpec((tm, tn), lambda i, j: (i, j)),
        out_shape=jax.ShapeDtypeStruct((M, N), F32),
        compiler_params=_cparams(("parallel", "parallel")),
        name="in_proj",
    )(h, w)


def _mm_res_kernel(*refs, n_lhs, coef):
    a_refs = refs[:n_lhs]
    w_refs = refs[n_lhs:2 * n_lhs]
    x_ref, g_ref, o_ref = refs[2 * n_lhs:]
    y = jnp.dot(a_refs[0][...], w_refs[0][...], preferred_element_type=F32)
    for a_ref, w_ref in zip(a_refs[1:], w_refs[1:]):
        y = y + jnp.dot(a_ref[...], w_ref[...], preferred_element_type=F32)
    o_ref[...] = x_ref[...] + (coef * g_ref[...]) * y


def _mm_res(a_list, w_list, x, gate, coef, name):
    M, N = x.shape
    ktot = sum(a.shape[1] for a in a_list)
    tm = _pick(M, (1024, 512, 256, 128, 64, 32, 16, 8)) if ktot <= 4096 else \
        _pick(M, (512, 256, 128, 64, 32, 16, 8))
    tn = _pick(N, (512, 256, 128)) if ktot <= 4096 else _pick(N, (256, 128))
    in_specs = [pl.BlockSpec((tm, a.shape[1]), lambda i, j: (i, 0)) for a in a_list]
    in_specs += [pl.BlockSpec((w.shape[0], tn), lambda i, j: (0, j)) for w in w_list]
    in_specs.append(pl.BlockSpec((tm, tn), lambda i, j: (i, j)))
    if gate.shape[0] == 1:
        in_specs.append(pl.BlockSpec((1, tn), lambda i, j: (0, j)))
    else:
        in_specs.append(pl.BlockSpec((tm, tn), lambda i, j: (i, j)))
    return pl.pallas_call(
        functools.partial(_mm_res_kernel, n_lhs=len(a_list), coef=coef),
        grid=(M // tm, N // tn),
        in_specs=in_specs,
        out_specs=pl.BlockSpec((tm, tn), lambda i, j: (i, j)),
        out_shape=jax.ShapeDtypeStruct((M, N), F32),
        compiler_params=_cparams(("parallel", "parallel")),
        name=name,
    )(*a_list, *w_list, x, gate)


class _Layout:
    def __init__(self, ha, kv, hb, hd, di):
        self.ha, self.kv, self.hb, self.hd, self.di = ha, kv, hb, hd, di
        self.q_a = 0
        self.k_a = self.q_a + ha * hd
        self.v_a = self.k_a + kv * hd
        self.q_i = self.v_a + kv * hd
        self.kiw = self.q_i + N_IDX_HEADS * di
        self.q_b = self.kiw + LANES
        self.k_b = self.q_b + hb * 2 * hd
        self.v_b = self.k_b + hb * 2 * hd
        self.end = self.v_b + hb * 2 * hd
        self.width = _round_up(self.end, 512)


def _rope_tables(hd, di):
    def table(d):
        r = d // ROT_FRAC
        half = r // 2
        inv = ROPE_THETA ** (-(jnp.arange(half, dtype=jnp.float32) * 2.0 / r))
        lane = np.arange(LANES) % d
        idx = np.where(lane < half, lane, np.where(lane < r, lane - half, 0))
        invl = jnp.where(lane < r, inv[idx], 0.0)
        m_hi = ((lane >= half) & (lane < r)).astype(np.float32)
        m_lo = -(lane < half).astype(np.float32)
        return jnp.stack([invl, jnp.asarray(m_hi), jnp.asarray(m_lo)])[:, None, :], half
    th, half_h = table(hd)
    ti, half_i = table(di)
    return th, half_h, ti, half_i


def _post_kernel(z_ref, gqa_ref, gka_ref, gki_ref, gqb_ref, gkb_ref, th_ref, ti_ref,
                 ka_ref, va_ref, ki_ref, kb_ref, vb_ref,
                 qa16_ref, ka16_ref, va16_ref, qi16_ref, kiw_ref, kiw16_ref, qb16_ref, kb16_ref, vb16_ref,
                 *, lay, ts, pos_off, half_h, half_i, scale):
    hd, di = lay.hd, lay.di
    row = lax.broadcasted_iota(jnp.int32, (ts, LANES), 0)
    pos = (pos_off + pl.program_id(1) * ts + row).astype(F32)

    def rope_fn(t_ref, half):
        ang = pos * t_ref[0]
        cosv = jnp.cos(ang)
        sinv = jnp.sin(ang)
        s_hi = sinv * t_ref[1]
        s_lo = sinv * t_ref[2]

        def rope(x):
            return x * cosv + pltpu.roll(x, half, 1) * s_hi + pltpu.roll(x, LANES - half, 1) * s_lo
        return rope

    rope_h = rope_fn(th_ref, half_h)
    rope_i = rope_fn(ti_ref, half_i)

    def hnorm(x, g):
        ms = jnp.mean(x * x, axis=-1, keepdims=True)
        return x * lax.rsqrt(ms + EPS) * g

    def slab(off, h):
        return z_ref[0, :, off + h * hd: off + (h + 1) * hd]

    def cols(h):
        return slice(h * hd, (h + 1) * hd)

    for h in range(lay.ha):
        q = rope_h(hnorm(slab(lay.q_a, h), gqa_ref[...]))
        qa16_ref[0, :, cols(h)] = (q * scale).astype(BF16)
    for h in range(lay.kv):
        k = rope_h(hnorm(slab(lay.k_a, h), gka_ref[...]))
        ka_ref[0, :, cols(h)] = k
        ka16_ref[0, :, cols(h)] = k.astype(BF16)
        v = slab(lay.v_a, h)
        va_ref[0, :, cols(h)] = v
        va16_ref[0, :, cols(h)] = v.astype(BF16)
    for h in range(N_IDX_HEADS * di // LANES):
        qi16_ref[0, :, cols(h)] = rope_i(slab(lay.q_i, h)).astype(BF16)
    x = slab(lay.kiw, 0)
    lane = lax.broadcasted_iota(jnp.int32, (ts, LANES), 1)
    is_k = lane < di
    ms = jnp.sum(jnp.where(is_k, x * x, 0.0), axis=-1, keepdims=True) * (1.0 / di)
    ki = rope_i(x * lax.rsqrt(ms + EPS) * gki_ref[...])
    ki_ref[0] = ki[:, :di]
    kiw = jnp.where(is_k, ki, x)
    kiw_ref[0] = kiw
    kiw16_ref[0] = kiw.astype(BF16)
    for h in range(2 * lay.hb):
        q = rope_h(hnorm(slab(lay.q_b, h), gqb_ref[...]))
        qb16_ref[0, :, cols(h)] = (q * scale).astype(BF16)
        k = rope_h(hnorm(slab(lay.k_b, h), gkb_ref[...]))
        kb_ref[0, :, cols(h)] = k
        kb16_ref[0, :, cols(h)] = k.astype(BF16)
        v = slab(lay.v_b, h)
        vb_ref[0, :, cols(h)] = v
        vb16_ref[0, :, cols(h)] = v.astype(BF16)


def _post(z, lay, gains, tables, pos_off):
    B, S, W = z.shape
    hd, di = lay.hd, lay.di
    ts = _pick(S, (128, 64, 32, 16, 8))
    th, half_h, ti, half_i = tables
    wa, wkv, wqi, wb = lay.ha * hd, lay.kv * hd, N_IDX_HEADS * di, lay.hb * 2 * hd

    def rows(width):
        return pl.BlockSpec((1, ts, width), lambda b, s: (b, s, 0))

    def const(a):
        return pl.BlockSpec(a.shape, lambda b, s: (0,) * a.ndim)

    out_widths = [(wkv, F32), (wkv, F32), (di, F32), (wb, F32), (wb, F32),
                  (wa, BF16), (wkv, BF16), (wkv, BF16), (wqi, BF16), (LANES, F32), (LANES, BF16),
                  (wb, BF16), (wb, BF16), (wb, BF16)]
    return pl.pallas_call(
        functools.partial(_post_kernel, lay=lay, ts=ts, pos_off=pos_off, half_h=half_h, half_i=half_i,
                          scale=hd ** -0.5 * math.log2(math.e)),
        grid=(B, S // ts),
        in_specs=[rows(W)] + [const(g) for g in gains] + [const(th), const(ti)],
        out_specs=[rows(w) for w, _ in out_widths],
        out_shape=[jax.ShapeDtypeStruct((B, S, w), dt) for w, dt in out_widths],
        compiler_params=_cparams(("parallel", "parallel")),
        name="head_norm_rope",
    )(z, *gains, th, ti)


def _tile_bounds(p0, tq, tk, n_keys):
    a_full = jnp.minimum((p0 // CHUNK + 1) * CHUNK, n_keys)
    a_tot = jnp.minimum(((p0 + tq - 1) // CHUNK + 1) * CHUNK, n_keys)
    return a_full // tk, (a_tot + tk - 1) // tk


def _allowed(j, p0, tq, tk, n_keys):
    shift = CHUNK.bit_length() - 1
    kpos = j * tk + lax.broadcasted_iota(jnp.int32, (tq, tk), 1)
    qpos = p0 + lax.broadcasted_iota(jnp.int32, (tq, tk), 0)
    return (lax.shift_right_logical(kpos, shift) <= lax.shift_right_logical(qpos, shift)) & (kpos < n_keys)


def _lane_tile_list(parts):
    return parts[0] if len(parts) == 1 else jnp.concatenate(parts, axis=1)


def _lane_tile(x, reps):
    return _lane_tile_list([x] * reps)


def _key_tile(lp):
    return 1024 if lp % 1024 == 0 else 256


def _softmax_step(s, v, m_ref, l_ref, acc_ref, idx):
    tk = s.shape[1]
    wv = acc_ref.shape[-1]
    m_old = m_ref[idx]
    m_new = jnp.maximum(m_old, jnp.max(s, axis=1, keepdims=True))
    alpha = jnp.exp2(m_old - m_new)
    p = jnp.exp2(s - _lane_tile(m_new, tk // LANES))
    if l_ref is not None:
        l_ref[idx] = alpha * l_ref[idx] + jnp.sum(p, axis=1, keepdims=True)
    acc_ref[idx] = _lane_tile(alpha, wv // LANES) * acc_ref[idx] + jnp.dot(
        p.astype(BF16), v, preferred_element_type=F32)
    m_ref[idx] = m_new


def _dsa_kernel(qa_ref, qi_ref, wq_ref, ka_ref, va_ref, ki_ref, o_ref,
                keys_ref, wexp_ref, qih_ref, tau_ref, bias_ref, m_ref, acc_ref,
                *, tq, tk, n_keys, pos_off, topk, ha, kv, hd, di):
    p0 = pos_off + pl.program_id(1) * tq
    n_full, n_tot = _tile_bounds(p0, tq, tk, n_keys)
    reps = tk // LANES
    group = ha // kv
    int_min = jnp.int32(-2 ** 31)

    lane = lax.broadcasted_iota(jnp.int32, (tq, LANES), 1)
    w_scale = (N_IDX_HEADS ** -0.5) * (di ** -0.5)
    for h in range(N_IDX_HEADS):
        pair = qi_ref[0, :, (h * di // LANES) * LANES:(h * di // LANES + 1) * LANES].astype(F32)
        off = (h * di) % LANES
        if off:
            pair = pltpu.roll(pair, LANES - off, 1)
        qih_ref[h] = jnp.where(lane < di, pair, 0.0).astype(BF16)
        wexp_ref[h] = jnp.broadcast_to(wq_ref[0, :, di + h:di + h + 1] * w_scale, (tq, LANES))

    def key_slice(j):
        return pl.ds(pl.multiple_of(j * tk, tk), tk)

    def score_tile(j, masked):
        kt = ki_ref[0, key_slice(j), :]
        parts = [jnp.zeros((tq, LANES), F32)] * reps
        for h in range(N_IDX_HEADS):
            lg = lax.dot_general(qih_ref[h], kt, NT_DIMS, preferred_element_type=F32)
            w = wexp_ref[h]
            parts = [parts[c] + jnp.maximum(lg[:, c * LANES:(c + 1) * LANES], 0.0) * w for c in range(reps)]
        sc = _lane_tile_list(parts)
        if masked:
            sc = jnp.where(_allowed(j, p0, tq, tk, n_keys), sc, -jnp.inf)
        bits = pltpu.bitcast(sc, jnp.int32)
        keys_ref[j] = jnp.where(bits < 0, bits ^ jnp.int32(0x7FFFFFFF), bits)

    def score_full(j, c):
        score_tile(j, False)
        return c

    def score_masked(j, c):
        score_tile(j, True)
        return c

    lax.fori_loop(0, n_full, score_full, 0)
    lax.fori_loop(n_full, n_tot, score_masked, 0)

    def count_ge(cand):
        def body(j, acc):
            t = keys_ref[j]
            for c in range(reps):
                acc = acc + jnp.where(t[:, c * LANES:(c + 1) * LANES] >= cand, 1.0, 0.0)
            return acc
        acc = lax.fori_loop(0, n_tot, body, jnp.zeros((tq, LANES), F32))
        return jnp.sum(acc, axis=1, keepdims=True)

    kf = float(topk)
    zero = jnp.zeros((tq, LANES), jnp.int32)
    prefix = jnp.where(count_ge(zero) >= kf, zero, int_min)

    def bit_body(b, prefix):
        cand = prefix + lax.shift_left(jnp.int32(1), 30 - b)
        return jnp.where(count_ge(cand) >= kf, cand, prefix)

    prefix = lax.fori_loop(0, 31, bit_body, prefix)
    tau_ref[...] = prefix

    m_ref[...] = jnp.full(m_ref.shape, NEG, F32)
    acc_ref[...] = jnp.zeros(acc_ref.shape, F32)
    ones = jnp.ones((tk, hd), BF16)

    def attn_tile(j, masked):
        sel = keys_ref[j] >= _lane_tile(tau_ref[...], reps)
        if masked:
            sel = sel & _allowed(j, p0, tq, tk, n_keys)
        bias_ref[...] = jnp.where(sel, 0.0, NEG)
        for h in range(ha):
            g = h // group
            k = ka_ref[0, key_slice(j), g * hd:(g + 1) * hd]
            v1 = jnp.concatenate([va_ref[0, key_slice(j), g * hd:(g + 1) * hd], ones], axis=1)
            s = lax.dot_general(qa_ref[0, :, h * hd:(h + 1) * hd], k, NT_DIMS, preferred_element_type=F32)
            _softmax_step(s + bias_ref[...], v1, m_ref, None, acc_ref, h)

    def attn_full(j, c):
        attn_tile(j, False)
        return c

    def attn_masked(j, c):
        attn_tile(j, True)
        return c

    lax.fori_loop(0, n_full, attn_full, 0)
    lax.fori_loop(n_full, n_tot, attn_masked, 0)
    for h in range(ha):
        o_ref[0, :, h * hd:(h + 1) * hd] = (acc_ref[h, :, :hd] / acc_ref[h, :, hd:]).astype(o_ref.dtype)


def _dsa(qa, qi, kiw_q, ka, va, kiw_k, n_keys, pos_off, kv, hd, di):
    B, S, wa = qa.shape
    lp = ka.shape[1]
    ha = wa // hd
    tq = _pick(S, (128, 64, 32, 16, 8))
    tk = _key_tile(lp)
    topk = min(TOPK_MAX, n_keys // 4)
    kern = functools.partial(_dsa_kernel, tq=tq, tk=tk, n_keys=n_keys, pos_off=pos_off, topk=topk,
                             ha=ha, kv=kv, hd=hd, di=di)

    def rows(width):
        return pl.BlockSpec((1, tq, width), lambda b, i: (b, i, 0))

    def resident(width):
        return pl.BlockSpec((1, lp, width), lambda b, i: (b, 0, 0), pipeline_mode=pl.Buffered(1))

    return pl.pallas_call(
        kern,
        grid=(B, S // tq),
        in_specs=[rows(wa), rows(qi.shape[2]), rows(LANES),
                  resident(kv * hd), resident(kv * hd), resident(LANES)],
        out_specs=rows(wa),
        out_shape=jax.ShapeDtypeStruct((B, S, wa), BF16),
        scratch_shapes=[pltpu.VMEM((lp // tk, tq, tk), jnp.int32),
                        pltpu.VMEM((N_IDX_HEADS, tq, LANES), F32),
                        pltpu.VMEM((N_IDX_HEADS, tq, LANES), BF16),
                        pltpu.VMEM((tq, LANES), jnp.int32),
                        pltpu.VMEM((tq, tk), F32),
                        pltpu.VMEM((ha, tq, LANES), F32),
                        pltpu.VMEM((ha, tq, 2 * hd), F32)],
        compiler_params=_cparams(("parallel", "arbitrary")),
        name="dsa_attention",
    )(qa, qi, kiw_q, ka, va, kiw_k)


def _diff_kernel(q_ref, k_ref, v_ref, lq1_ref, lk1_ref, lq2_ref, lk2_ref, sg_ref, o_ref,
                 m_ref, l_ref, acc_ref, s_ref, *, tq, tk, n_keys, pos_off, hd, lam_init):
    p0 = pos_off + pl.program_id(2) * tq
    n_full, n_tot = _tile_bounds(p0, tq, tk, n_keys)
    db = 2 * hd
    m_ref[...] = jnp.full(m_ref.shape, NEG, F32)
    l_ref[...] = jnp.zeros(l_ref.shape, F32)
    acc_ref[...] = jnp.zeros(acc_ref.shape, F32)

    def key_slice(j):
        return pl.ds(pl.multiple_of(j * tk, tk), tk)

    def scores(j, c):
        return lax.dot_general(q_ref[0, :, c * hd:(c + 1) * hd], k_ref[0, key_slice(j), c * hd:(c + 1) * hd],
                               NT_DIMS, preferred_element_type=F32)

    def produce(j, slot):
        for c in range(2):
            s_ref[slot, c] = scores(j, c)

    def consume(j, slot):
        v = v_ref[0, key_slice(j), :]
        for c in range(2):
            _softmax_step(s_ref[slot, c], v, m_ref, l_ref, acc_ref, c)

    n_pairs = jnp.maximum(n_full - 1, 0) // 2

    @pl.when(n_full > 0)
    def _():
        produce(0, 0)

    def pair_body(t, carry):
        produce(2 * t + 1, 1)
        consume(2 * t, 0)
        produce(2 * t + 2, 0)
        consume(2 * t + 1, 1)
        return carry

    lax.fori_loop(0, n_pairs, pair_body, 0)
    left = n_full - 2 * n_pairs

    @pl.when(left == 2)
    def _():
        produce(2 * n_pairs + 1, 1)
        consume(2 * n_pairs, 0)
        consume(2 * n_pairs + 1, 1)

    @pl.when(left == 1)
    def _():
        consume(2 * n_pairs, 0)

    def tile_masked(j, carry):
        v = v_ref[0, key_slice(j), :]
        ok = _allowed(j, p0, tq, tk, n_keys)
        for c in range(2):
            _softmax_step(jnp.where(ok, scores(j, c), NEG), v, m_ref, l_ref, acc_ref, c)
        return carry

    lax.fori_loop(n_full, n_tot, tile_masked, 0)

    lam = (jnp.exp(jnp.sum(lq1_ref[...] * lk1_ref[...], axis=-1, keepdims=True))
           - jnp.exp(jnp.sum(lq2_ref[...] * lk2_ref[...], axis=-1, keepdims=True)) + lam_init)
    reps = db // LANES
    o = acc_ref[0] / _lane_tile(l_ref[0], reps) - lam * (acc_ref[1] / _lane_tile(l_ref[1], reps))
    ms = jnp.mean(o * o, axis=-1, keepdims=True)
    o_ref[0] = (o * lax.rsqrt(ms + EPS) * sg_ref[...] * (1.0 - lam_init)).astype(o_ref.dtype)


def _diff(qb, kb, vb, lams, subln_g, n_keys, pos_off, hd, lam_init):
    B, S, wb = qb.shape
    lp = kb.shape[1]
    db = 2 * hd
    hb = wb // db
    tq = _pick(S, (256, 128, 64, 32, 16, 8))
    tk = _key_tile(lp)
    kern = functools.partial(_diff_kernel, tq=tq, tk=tk, n_keys=n_keys, pos_off=pos_off, hd=hd,
                             lam_init=lam_init)
    vec = pl.BlockSpec((1, hd), lambda b, h, i: (0, 0))
    return pl.pallas_call(
        kern,
        grid=(B, hb, S // tq),
        in_specs=[pl.BlockSpec((1, tq, db), lambda b, h, i: (b, i, h)),
                  pl.BlockSpec((1, lp, db), lambda b, h, i: (b, 0, h)),
                  pl.BlockSpec((1, lp, db), lambda b, h, i: (b, 0, h)),
                  vec, vec, vec, vec,
                  pl.BlockSpec((1, db), lambda b, h, i: (0, 0))],
        out_specs=pl.BlockSpec((1, tq, db), lambda b, h, i: (b, i, h)),
        out_shape=jax.ShapeDtypeStruct((B, S, wb), BF16),
        scratch_shapes=[pltpu.VMEM((2, tq, LANES), F32),
                        pltpu.VMEM((2, tq, LANES), F32),
                        pltpu.VMEM((2, tq, db), F32),
                        pltpu.VMEM((2, 2, tq, tk), F32)],
        compiler_params=_cparams(("parallel", "parallel", "arbitrary")),
        name="diff_attention",
    )(qb, kb, vb, *lams, subln_g)


def _pad_rows(a, lp):
    return jnp.pad(a, ((0, 0), (0, lp - a.shape[1]), (0, 0)))


def _prep_w_in(w_in, lay, d_idx_w):
    ha, kv, hb, hd, di = lay.ha, lay.kv, lay.hb, lay.hd, lay.di
    widths = (ha * hd, kv * hd, kv * hd, N_IDX_HEADS * di, di, d_idx_w, hb * 2 * hd, hb * 2 * hd, hb * 2 * hd)
    offs = np.cumsum((0,) + widths)
    seg = [w_in[:, offs[i]:offs[i + 1]] for i in range(len(widths))]
    d = w_in.shape[0]
    pad_kiw = jnp.zeros((d, LANES - di - d_idx_w), w_in.dtype)
    pad_end = jnp.zeros((d, lay.width - lay.end), w_in.dtype)
    return jnp.concatenate(seg[:6] + [pad_kiw] + seg[6:] + [pad_end], axis=1).astype(BF16)


def _layer(x, B, S, mods, past, pos_off, lw, li):
    (sh1, sc1, gt1, sh2, sc2, gt2, sh3, sc3, gt3) = mods
    D = x.shape[1]
    lay = lw["lay"]
    hd, di, kv = lay.hd, lay.di, lay.kv

    h = _norm_mod(x, lw["norm_ffn1"], sc1, sh1)
    a = _mm_up(h, lw["w1_1"], lw["w3_1"])
    x = _mm_res([a], [lw["w2_1"]], x, gt1, 0.5, "ffn_down")

    h = _norm_mod(x, lw["norm_attn"], sc2, sh2)
    z = _mm_plain(h, lw["w_in"]).reshape(B, S, lay.width)
    (k_a, v_a, k_i, k_b, v_b, qa16, ka16, va16, qi16, kiw, kiw16, qb16, kb16, vb16) = _post(
        z, lay, lw["gains"], lw["tables"], pos_off)

    if past is not None:
        ck_a, cv_a, ck_i, ck_b, cv_b = past
        P = ck_a.shape[1]
        flat = lambda c: c.reshape(B, P, -1).astype(BF16)
        ka16 = jnp.concatenate([flat(ck_a), ka16], axis=1)
        va16 = jnp.concatenate([flat(cv_a), va16], axis=1)
        ck_i16 = jnp.pad(flat(ck_i), ((0, 0), (0, 0), (0, LANES - di)))
        kiw16_k = jnp.concatenate([ck_i16, kiw16], axis=1)
        kb16 = jnp.concatenate([flat(ck_b), kb16], axis=1)
        vb16 = jnp.concatenate([flat(cv_b), vb16], axis=1)
        n_keys = P + S
    else:
        kiw16_k = kiw16
        n_keys = S
    lp = _round_up(n_keys, 256)
    if lp != n_keys:
        ka16, va16, kiw16_k, kb16, vb16 = (_pad_rows(t, lp) for t in (ka16, va16, kiw16_k, kb16, vb16))

    oa = _dsa(qa16, qi16, kiw, ka16, va16, kiw16_k, n_keys, pos_off, kv, hd, di)
    ob = _diff(qb16, kb16, vb16, lw["lams"], lw["subln_g"], n_keys, pos_off, hd, _lambda_init(li))
    wa = oa.shape[2]
    x = _mm_res([oa.reshape(B * S, wa), ob.reshape(B * S, -1)], [lw["w_out"][:wa], lw["w_out"][wa:]],
                x, gt2, 1.0, "out_proj")

    h = _norm_mod(x, lw["norm_ffn2"], sc3, sh3)
    a = _mm_up(h, lw["w1_2"], lw["w3_2"])
    x = _mm_res([a], [lw["w2_2"]], x, gt3, 0.5, "ffn_down")
    return x, (k_a, v_a, k_i, k_b, v_b)


def kernel(x_prompt, x_sample, cache_k_a, cache_v_a, cache_k_idx, cache_k_b, cache_v_b, c_prompt, c_sample, w_ada, b_ada, norm_ffn1, w1_ffn1, w3_ffn1, w2_ffn1, norm_attn, w_in, qnorm_a, knorm_a, knorm_idx, qnorm_b, knorm_b, lam_q1, lam_k1, lam_q2, lam_k2, subln_b, w_out, norm_ffn2, w1_ffn2, w3_ffn2, w2_ffn2):
    depth = w_ada.shape[0]
    Bp, Sp, D = x_prompt.shape
    Bs, Ss, _ = x_sample.shape
    P = cache_k_a.shape[2]
    kv, hd = cache_k_a.shape[3], cache_k_a.shape[4]
    di = cache_k_idx.shape[3]
    hb = cache_k_b.shape[3]
    db = cache_v_b.shape[4]
    ha = (w_out.shape[1] - hb * db) // hd
    lay = _Layout(ha, kv, hb, hd, di)
    tables = _rope_tables(hd, di)

    xp = x_prompt.reshape(Bp * Sp, D)
    xs = x_sample.reshape(Bs * Ss, D)
    n_c = Bp + Bs
    c_all = jnp.pad(jnp.concatenate([c_prompt, c_sample], axis=0), ((0, _round_up(n_c, 8) - n_c), (0, 0)))

    rows_p, rows_s = [], []
    for l in range(depth):
        row = lambda a: a[l][None, :]
        gki = jnp.pad(knorm_idx[l], (0, LANES - di))[None, :]
        lw = dict(
            lay=lay, tables=tables,
            norm_ffn1=row(norm_ffn1), norm_attn=row(norm_attn), norm_ffn2=row(norm_ffn2),
            w1_1=w1_ffn1[l].astype(BF16), w3_1=w3_ffn1[l].astype(BF16), w2_1=w2_ffn1[l].astype(BF16),
            w1_2=w1_ffn2[l].astype(BF16), w3_2=w3_ffn2[l].astype(BF16), w2_2=w2_ffn2[l].astype(BF16),
            w_in=_prep_w_in(w_in[l], lay, N_IDX_HEADS), w_out=w_out[l].astype(BF16),
            gains=(row(qnorm_a), row(knorm_a), gki, row(qnorm_b), row(knorm_b)),
            lams=(row(lam_q1), row(lam_k1), row(lam_q2), row(lam_k2)),
            subln_g=row(subln_b),
        )
        mod = _ada(c_all, w_ada[l], b_ada[l][None, :])

        def mods_for(lo, nb, s):
            m = mod[lo:lo + nb].reshape(nb, N_MOD, D)
            if nb == 1:
                return tuple(m[:, i] for i in range(N_MOD))
            return tuple(jnp.repeat(m[:, i], s, axis=0) for i in range(N_MOD))

        xp, rp = _layer(xp, Bp, Sp, mods_for(0, Bp, Sp), None, 0, lw, l)
        past = (cache_k_a[l], cache_v_a[l], cache_k_idx[l], cache_k_b[l], cache_v_b[l])
        xs, rs = _layer(xs, Bs, Ss, mods_for(Bp, Bs, Ss), past, P, lw, l)
        rows_p.append(rp)
        rows_s.append(rs)

    def stack(rows, idx, B, S, tail):
        return jnp.stack([r[idx].reshape((B, S) + tail) for r in rows])

    tails = ((kv, hd), (kv, hd), (di,), (hb, 2, hd), (hb, db))
    outs_p = tuple(stack(rows_p, i, Bp, Sp, tails[i]) for i in range(5))
    outs_s = tuple(stack(rows_s, i, Bs, Ss, tails[i]) for i in range(5))
    return (xp.reshape(Bp, Sp, D), xs.reshape(Bs, Ss, D)) + outs_p + outs_s
```

```python
import functools
import math

import numpy as np
import jax
import jax.numpy as jnp
from jax import lax
from jax.experimental import pallas as pl
from jax.experimental.pallas import tpu as pltpu

F32 = jnp.float32
BF16 = jnp.bfloat16

CHUNK = 64
N_IDX_HEADS = 16
TOPK_MAX = 256
ROPE_THETA = 500000.0
ROT_FRAC = 4
EPS = 1e-6
N_MOD = 9

LANES = 128
V7X_VMEM_BYTES = 64 * 1024 * 1024
VMEM_LIMIT = V7X_VMEM_BYTES * 7 // 8
NEG = -1e30
NT_DIMS = (((1,), (1,)), ((), ()))


def _lambda_init(layer_idx):
    return 0.8 - 0.6 * math.exp(-0.3 * layer_idx)


def _pick(n, candidates):
    for c in candidates:
        if n % c == 0:
            return c
    return n


def _cparams(semantics):
    return pltpu.CompilerParams(dimension_semantics=semantics, vmem_limit_bytes=VMEM_LIMIT)


def _round_up(n, m):
    return (n + m - 1) // m * m


def _ada_kernel(c_ref, w_ref, b_ref, o_ref):
    c = c_ref[...]
    s = (c * jax.nn.sigmoid(c)).astype(BF16)
    o_ref[...] = jnp.dot(s, w_ref[...].astype(BF16), preferred_element_type=F32) + b_ref[...]


def _ada(c, w, b):
    R, D = c.shape
    N = w.shape[1]
    tn = _pick(N, (512, 256, 128))
    return pl.pallas_call(
        _ada_kernel,
        grid=(N // tn,),
        in_specs=[pl.BlockSpec((R, D), lambda j: (0, 0)),
                  pl.BlockSpec((D, tn), lambda j: (0, j)),
                  pl.BlockSpec((1, tn), lambda j: (0, j))],
        out_specs=pl.BlockSpec((R, tn), lambda j: (0, j)),
        out_shape=jax.ShapeDtypeStruct((R, N), F32),
        compiler_params=_cparams(("parallel",)),
        name="ada_matvec",
    )(c, w, b)


def _norm_mod_kernel(x_ref, g_ref, sc_ref, sh_ref, o_ref):
    x = x_ref[...]
    ms = jnp.mean(x * x, axis=-1, keepdims=True)
    y = x * lax.rsqrt(ms + EPS) * g_ref[...]
    o_ref[...] = (y * (1.0 + sc_ref[...]) + sh_ref[...]).astype(o_ref.dtype)


def _norm_mod(x, g, scale, shift):
    M, D = x.shape
    tm = _pick(M, (256, 128, 64, 32, 16, 8))

    def mspec(a):
        if a.shape[0] == 1:
            return pl.BlockSpec((1, D), lambda i: (0, 0))
        return pl.BlockSpec((tm, D), lambda i: (i, 0))

    return pl.pallas_call(
        _norm_mod_kernel,
        grid=(M // tm,),
        in_specs=[pl.BlockSpec((tm, D), lambda i: (i, 0)),
                  pl.BlockSpec((1, D), lambda i: (0, 0)),
                  mspec(scale), mspec(shift)],
        out_specs=pl.BlockSpec((tm, D), lambda i: (i, 0)),
        out_shape=jax.ShapeDtypeStruct((M, D), BF16),
        compiler_params=_cparams(("parallel",)),
        name="norm_mod",
    )(x, g, scale, shift)


def _mm_up_kernel(h_ref, w1_ref, w3_ref, o_ref):
    h = h_ref[...]
    u = jnp.dot(h, w1_ref[...].astype(BF16), preferred_element_type=F32)
    v = jnp.dot(h, w3_ref[...].astype(BF16), preferred_element_type=F32)
    o_ref[...] = (u * jax.nn.sigmoid(u) * v).astype(o_ref.dtype)


def _mm_up(h, w1, w3):
    M, K = h.shape
    F = w1.shape[1]
    tm = _pick(M, (1024, 512, 256, 128, 64, 32, 16, 8))
    tn = _pick(F, (256, 128))
    return pl.pallas_call(
        _mm_up_kernel,
        grid=(M // tm, F // tn),
        in_specs=[pl.BlockSpec((tm, K), lambda i, j: (i, 0)),
                  pl.BlockSpec((K, tn), lambda i, j: (0, j)),
                  pl.BlockSpec((K, tn), lambda i, j: (0, j))],
        out_specs=pl.BlockSpec((tm, tn), lambda i, j: (i, j)),
        out_shape=jax.ShapeDtypeStruct((M, F), BF16),
        compiler_params=_cparams(("parallel", "parallel")),
        name="ffn_up",
    )(h, w1, w3)


def _mm_plain_kernel(h_ref, w_ref, o_ref):
    o_ref[...] = jnp.dot(h_ref[...], w_ref[...], preferred_element_type=F32)


def _mm_plain(h, w):
    M, K = h.shape
    N = w.shape[1]
    tm = _pick(M, (1024, 512, 256, 128, 64, 32, 16, 8))
    tn = _pick(N, (512, 256, 128))
    return pl.pallas_call(
        _mm_plain_kernel,
        grid=(M // tm, N // tn),
        in_specs=[pl.BlockSpec((tm, K), lambda i, j: (i, 0)),
                  pl.BlockSpec((K, tn), lambda i, j: (0, j))],
        out_specs=pl.BlockSpec((tm, tn), lambda i, j: (i, j)),
        out_shape=jax.ShapeDtypeStruct((M, N), F32),
        compiler_params=_cparams(("parallel", "parallel")),
        name="in_proj",
    )(h, w)


def _mm_res_kernel(*refs, n_lhs, coef):
    a_refs = refs[:n_lhs]
    w_refs = refs[n_lhs:2 * n_lhs]
    x_ref, g_ref, o_ref = refs[2 * n_lhs:]
    y = jnp.dot(a_refs[0][...], w_refs[0][...], preferred_element_type=F32)
    for a_ref, w_ref in zip(a_refs[1:], w_refs[1:]):
        y = y + jnp.dot(a_ref[...], w_ref[...], preferred_element_type=F32)
    o_ref[...] = x_ref[...] + (coef * g_ref[...]) * y


def _mm_res(a_list, w_list, x, gate, coef, name):
    M, N = x.shape
    ktot = sum(a.shape[1] for a in a_list)
    tm = _pick(M, (1024, 512, 256, 128, 64, 32, 16, 8))
    tn = _pick(N, (512, 256, 128)) if ktot <= 4096 else _pick(N, (256, 128))
    lhs_mode = {} if ktot <= 4096 else dict(pipeline_mode=pl.Buffered(1))
    in_specs = [pl.BlockSpec((tm, a.shape[1]), lambda i, j: (i, 0), **lhs_mode) for a in a_list]
    in_specs += [pl.BlockSpec((w.shape[0], tn), lambda i, j: (0, j)) for w in w_list]
    in_specs.append(pl.BlockSpec((tm, tn), lambda i, j: (i, j)))
    if gate.shape[0] == 1:
        in_specs.append(pl.BlockSpec((1, tn), lambda i, j: (0, j)))
    else:
        in_specs.append(pl.BlockSpec((tm, tn), lambda i, j: (i, j)))
    return pl.pallas_call(
        functools.partial(_mm_res_kernel, n_lhs=len(a_list), coef=coef),
        grid=(M // tm, N // tn),
        in_specs=in_specs,
        out_specs=pl.BlockSpec((tm, tn), lambda i, j: (i, j)),
        out_shape=jax.ShapeDtypeStruct((M, N), F32),
        compiler_params=_cparams(("parallel", "parallel")),
        name=name,
    )(*a_list, *w_list, x, gate)


class _Layout:
    def __init__(self, ha, kv, hb, hd, di):
        self.ha, self.kv, self.hb, self.hd, self.di = ha, kv, hb, hd, di
        self.q_a = 0
        self.k_a = self.q_a + ha * hd
        self.v_a = self.k_a + kv * hd
        self.q_i = self.v_a + kv * hd
        self.kiw = self.q_i + N_IDX_HEADS * di
        self.q_b = self.kiw + LANES
        self.k_b = self.q_b + hb * 2 * hd
        self.v_b = self.k_b + hb * 2 * hd
        self.end = self.v_b + hb * 2 * hd
        self.width = _round_up(self.end, 512)


def _rope_tables(hd, di):
    def table(d):
        r = d // ROT_FRAC
        half = r // 2
        inv = ROPE_THETA ** (-(jnp.arange(half, dtype=jnp.float32) * 2.0 / r))
        lane = np.arange(LANES) % d
        idx = np.where(lane < half, lane, np.where(lane < r, lane - half, 0))
        invl = jnp.where(lane < r, inv[idx], 0.0)
        m_hi = ((lane >= half) & (lane < r)).astype(np.float32)
        m_lo = -(lane < half).astype(np.float32)
        return jnp.stack([invl, jnp.asarray(m_hi), jnp.asarray(m_lo)])[:, None, :], half
    th, half_h = table(hd)
    ti, half_i = table(di)
    return th, half_h, ti, half_i


def _post_kernel(z_ref, gqa_ref, gka_ref, gki_ref, gqb_ref, gkb_ref, th_ref, ti_ref,
                 ka_ref, va_ref, ki_ref, kb_ref, vb_ref,
                 qa16_ref, ka16_ref, va16_ref, qi16_ref, kiw_ref, kiw16_ref, qb16_ref, kb16_ref, vb16_ref,
                 *, lay, ts, pos_off, half_h, half_i, scale):
    hd, di = lay.hd, lay.di
    row = lax.broadcasted_iota(jnp.int32, (ts, LANES), 0)
    pos = (pos_off + pl.program_id(1) * ts + row).astype(F32)

    def rope_fn(t_ref, half):
        ang = pos * t_ref[0]
        cosv = jnp.cos(ang)
        sinv = jnp.sin(ang)
        s_hi = sinv * t_ref[1]
        s_lo = sinv * t_ref[2]

        def rope(x):
            return x * cosv + pltpu.roll(x, half, 1) * s_hi + pltpu.roll(x, LANES - half, 1) * s_lo
        return rope

    rope_h = rope_fn(th_ref, half_h)
    rope_i = rope_fn(ti_ref, half_i)

    def hnorm(x, g):
        ms = jnp.mean(x * x, axis=-1, keepdims=True)
        return x * lax.rsqrt(ms + EPS) * g

    def slab(off, h):
        return z_ref[0, :, off + h * hd: off + (h + 1) * hd]

    def cols(h):
        return slice(h * hd, (h + 1) * hd)

    for h in range(lay.ha):
        q = rope_h(hnorm(slab(lay.q_a, h), gqa_ref[...]))
        qa16_ref[0, :, cols(h)] = (q * scale).astype(BF16)
    for h in range(lay.kv):
        k = rope_h(hnorm(slab(lay.k_a, h), gka_ref[...]))
        ka_ref[0, :, cols(h)] = k
        ka16_ref[0, :, cols(h)] = k.astype(BF16)
        v = slab(lay.v_a, h)
        va_ref[0, :, cols(h)] = v
        va16_ref[0, :, cols(h)] = v.astype(BF16)
    for h in range(N_IDX_HEADS * di // LANES):
        qi16_ref[0, :, cols(h)] = rope_i(slab(lay.q_i, h)).astype(BF16)
    x = slab(lay.kiw, 0)
    lane = lax.broadcasted_iota(jnp.int32, (ts, LANES), 1)
    is_k = lane < di
    ms = jnp.sum(jnp.where(is_k, x * x, 0.0), axis=-1, keepdims=True) * (1.0 / di)
    ki = rope_i(x * lax.rsqrt(ms + EPS) * gki_ref[...])
    ki_ref[0] = ki[:, :di]
    kiw = jnp.where(is_k, ki, x)
    kiw_ref[0] = kiw
    kiw16_ref[0] = kiw.astype(BF16)
    for h in range(2 * lay.hb):
        q = rope_h(hnorm(slab(lay.q_b, h), gqb_ref[...]))
        qb16_ref[0, :, cols(h)] = (q * scale).astype(BF16)
        k = rope_h(hnorm(slab(lay.k_b, h), gkb_ref[...]))
        kb_ref[0, :, cols(h)] = k
        kb16_ref[0, :, cols(h)] = k.astype(BF16)
        v = slab(lay.v_b, h)
        vb_ref[0, :, cols(h)] = v
        vb16_ref[0, :, cols(h)] = v.astype(BF16)


def _post(z, lay, gains, tables, pos_off):
    B, S, W = z.shape
    hd, di = lay.hd, lay.di
    ts = _pick(S, (128, 64, 32, 16, 8))
    th, half_h, ti, half_i = tables
    wa, wkv, wqi, wb = lay.ha * hd, lay.kv * hd, N_IDX_HEADS * di, lay.hb * 2 * hd

    def rows(width):
        return pl.BlockSpec((1, ts, width), lambda b, s: (b, s, 0))

    def const(a):
        return pl.BlockSpec(a.shape, lambda b, s: (0,) * a.ndim)

    out_widths = [(wkv, F32), (wkv, F32), (di, F32), (wb, F32), (wb, F32),
                  (wa, BF16), (wkv, BF16), (wkv, BF16), (wqi, BF16), (LANES, F32), (LANES, BF16),
                  (wb, BF16), (wb, BF16), (wb, BF16)]
    return pl.pallas_call(
        functools.partial(_post_kernel, lay=lay, ts=ts, pos_off=pos_off, half_h=half_h, half_i=half_i,
                          scale=hd ** -0.5 * math.log2(math.e)),
        grid=(B, S // ts),
        in_specs=[rows(W)] + [const(g) for g in gains] + [const(th), const(ti)],
        out_specs=[rows(w) for w, _ in out_widths],
        out_shape=[jax.ShapeDtypeStruct((B, S, w), dt) for w, dt in out_widths],
        compiler_params=_cparams(("parallel", "parallel")),
        name="head_norm_rope",
    )(z, *gains, th, ti)


def _tile_bounds(p0, tq, tk, n_keys):
    a_full = jnp.minimum((p0 // CHUNK + 1) * CHUNK, n_keys)
    a_tot = jnp.minimum(((p0 + tq - 1) // CHUNK + 1) * CHUNK, n_keys)
    return a_full // tk, (a_tot + tk - 1) // tk


def _allowed(j, p0, tq, tk, n_keys):
    shift = CHUNK.bit_length() - 1
    kpos = j * tk + lax.broadcasted_iota(jnp.int32, (tq, tk), 1)
    qpos = p0 + lax.broadcasted_iota(jnp.int32, (tq, tk), 0)
    return (lax.shift_right_logical(kpos, shift) <= lax.shift_right_logical(qpos, shift)) & (kpos < n_keys)


def _lane_tile_list(parts):
    return parts[0] if len(parts) == 1 else jnp.concatenate(parts, axis=1)


def _lane_tile(x, reps):
    return _lane_tile_list([x] * reps)


def _key_tile(lp):
    return 1024 if lp % 1024 == 0 else 256


def _lane_tile_max(s):
    out = s[:, :LANES]
    for c in range(1, s.shape[1] // LANES):
        out = jnp.maximum(out, s[:, c * LANES:(c + 1) * LANES])
    return out


def _softmax_step(s, v, m_ref, l_ref, acc_ref, idx, tile_max=None):
    tk = s.shape[1]
    wv = acc_ref.shape[-1]
    m_old = m_ref[idx]
    if tile_max is None:
        tile_max = _lane_tile_max(s)
    m_new = jnp.maximum(m_old, jnp.max(tile_max, axis=1, keepdims=True))
    alpha = jnp.exp2(m_old - m_new)
    p = jnp.exp2(s - _lane_tile(m_new, tk // LANES))
    if l_ref is not None:
        l_ref[idx] = alpha * l_ref[idx] + jnp.sum(p, axis=1, keepdims=True)
    acc_ref[idx] = _lane_tile(alpha, wv // LANES) * acc_ref[idx] + jnp.dot(
        p.astype(BF16), v, preferred_element_type=F32)
    m_ref[idx] = m_new


def _dsa_kernel(qa_ref, qi_ref, wq_ref, ka_ref, va_ref, ki_ref, o_ref,
                keys_ref, wexp_ref, qih_ref, tau_ref, bias_ref, m_ref, acc_ref,
                *, tq, tk, n_keys, pos_off, topk, ha, kv, hd, di):
    p0 = pos_off + pl.program_id(1) * tq
    n_full, n_tot = _tile_bounds(p0, tq, tk, n_keys)
    reps = tk // LANES
    group = ha // kv
    int_min = jnp.int32(-2 ** 31)

    lane = lax.broadcasted_iota(jnp.int32, (tq, LANES), 1)
    w_scale = (N_IDX_HEADS ** -0.5) * (di ** -0.5)
    for h in range(N_IDX_HEADS):
        pair = qi_ref[0, :, (h * di // LANES) * LANES:(h * di // LANES + 1) * LANES].astype(F32)
        off = (h * di) % LANES
        if off:
            pair = pltpu.roll(pair, LANES - off, 1)
        qih_ref[h] = jnp.where(lane < di, pair, 0.0).astype(BF16)
        wexp_ref[h] = jnp.broadcast_to(wq_ref[0, :, di + h:di + h + 1] * w_scale, (tq, LANES))

    def key_slice(j):
        return pl.ds(pl.multiple_of(j * tk, tk), tk)

    def score_tile(j, masked):
        kt = ki_ref[0, key_slice(j), :]
        parts = [jnp.zeros((tq, LANES), F32)] * reps
        for h in range(N_IDX_HEADS):
            lg = lax.dot_general(qih_ref[h], kt, NT_DIMS, preferred_element_type=F32)
            w = wexp_ref[h]
            parts = [parts[c] + jnp.maximum(lg[:, c * LANES:(c + 1) * LANES], 0.0) * w for c in range(reps)]
        sc = _lane_tile_list(parts)
        if masked:
            sc = jnp.where(_allowed(j, p0, tq, tk, n_keys), sc, -jnp.inf)
        bits = pltpu.bitcast(sc, jnp.int32)
        keys_ref[j] = jnp.where(bits < 0, bits ^ jnp.int32(0x7FFFFFFF), bits)

    def score_full(j, c):
        score_tile(j, False)
        return c

    def score_masked(j, c):
        score_tile(j, True)
        return c

    lax.fori_loop(0, n_full, score_full, 0)
    lax.fori_loop(n_full, n_tot, score_masked, 0)

    def count_ge(cand):
        def body(j, acc):
            t = keys_ref[j]
            for c in range(reps):
                acc = jnp.where(t[:, c * LANES:(c + 1) * LANES] >= cand, acc + 1.0, acc)
            return acc
        acc = lax.fori_loop(0, n_tot, body, jnp.zeros((tq, LANES), F32))
        return jnp.sum(acc, axis=1, keepdims=True)

    kf = float(topk)
    zero = jnp.zeros((tq, LANES), jnp.int32)
    prefix = jnp.where(count_ge(zero) >= kf, zero, int_min)

    def bit_body(b, prefix):
        cand = prefix + lax.shift_left(jnp.int32(1), 30 - b)
        return jnp.where(count_ge(cand) >= kf, cand, prefix)

    prefix = lax.fori_loop(0, 31, bit_body, prefix)
    tau_ref[...] = prefix

    m_ref[...] = jnp.full(m_ref.shape, NEG, F32)
    acc_ref[...] = jnp.zeros(acc_ref.shape, F32)
    ones = jnp.ones((tk, hd), BF16)

    def attn_tile(j, masked):
        sel = keys_ref[j] >= _lane_tile(tau_ref[...], reps)
        if masked:
            sel = sel & _allowed(j, p0, tq, tk, n_keys)
        bias_ref[...] = jnp.where(sel, 0.0, NEG)
        for h in range(ha):
            g = h // group
            k = ka_ref[0, key_slice(j), g * hd:(g + 1) * hd]
            v1 = jnp.concatenate([va_ref[0, key_slice(j), g * hd:(g + 1) * hd], ones], axis=1)
            s = lax.dot_general(qa_ref[0, :, h * hd:(h + 1) * hd], k, NT_DIMS, preferred_element_type=F32)
            _softmax_step(s + bias_ref[...], v1, m_ref, None, acc_ref, h)

    def attn_full(j, c):
        attn_tile(j, False)
        return c

    def attn_masked(j, c):
        attn_tile(j, True)
        return c

    lax.fori_loop(0, n_full, attn_full, 0)
    lax.fori_loop(n_full, n_tot, attn_masked, 0)
    for h in range(ha):
        o_ref[0, :, h * hd:(h + 1) * hd] = (acc_ref[h, :, :hd] / acc_ref[h, :, hd:]).astype(o_ref.dtype)


def _dsa(qa, qi, kiw_q, ka, va, kiw_k, n_keys, pos_off, kv, hd, di):
    B, S, wa = qa.shape
    lp = ka.shape[1]
    ha = wa // hd
    tq = _pick(S, (128, 64, 32, 16, 8))
    tk = _key_tile(lp)
    topk = min(TOPK_MAX, n_keys // 4)
    kern = functools.partial(_dsa_kernel, tq=tq, tk=tk, n_keys=n_keys, pos_off=pos_off, topk=topk,
                             ha=ha, kv=kv, hd=hd, di=di)

    def rows(width):
        return pl.BlockSpec((1, tq, width), lambda b, i: (b, i, 0))

    def resident(width):
        return pl.BlockSpec((1, lp, width), lambda b, i: (b, 0, 0), pipeline_mode=pl.Buffered(1))

    return pl.pallas_call(
        kern,
        grid=(B, S // tq),
        in_specs=[rows(wa), rows(qi.shape[2]), rows(LANES),
                  resident(kv * hd), resident(kv * hd), resident(LANES)],
        out_specs=rows(wa),
        out_shape=jax.ShapeDtypeStruct((B, S, wa), BF16),
        scratch_shapes=[pltpu.VMEM((lp // tk, tq, tk), jnp.int32),
                        pltpu.VMEM((N_IDX_HEADS, tq, LANES), F32),
                        pltpu.VMEM((N_IDX_HEADS, tq, LANES), BF16),
                        pltpu.VMEM((tq, LANES), jnp.int32),
                        pltpu.VMEM((tq, tk), F32),
                        pltpu.VMEM((ha, tq, LANES), F32),
                        pltpu.VMEM((ha, tq, 2 * hd), F32)],
        compiler_params=_cparams(("parallel", "arbitrary")),
        name="dsa_attention",
    )(qa, qi, kiw_q, ka, va, kiw_k)


def _diff_kernel(q_ref, k_ref, v_ref, lq1_ref, lk1_ref, lq2_ref, lk2_ref, sg_ref, o_ref,
                 m_ref, l_ref, acc_ref, s_ref, mx_ref, *, tq, tk, n_keys, pos_off, hd, lam_init):
    p0 = pos_off + pl.program_id(2) * tq
    n_full, n_tot = _tile_bounds(p0, tq, tk, n_keys)
    db = 2 * hd
    m_ref[...] = jnp.full(m_ref.shape, NEG, F32)
    l_ref[...] = jnp.zeros(l_ref.shape, F32)
    acc_ref[...] = jnp.zeros(acc_ref.shape, F32)

    def key_slice(j):
        return pl.ds(pl.multiple_of(j * tk, tk), tk)

    def scores(j, c):
        return lax.dot_general(q_ref[0, :, c * hd:(c + 1) * hd], k_ref[0, key_slice(j), c * hd:(c + 1) * hd],
                               NT_DIMS, preferred_element_type=F32)

    def produce(j, slot):
        for c in range(2):
            s = scores(j, c)
            s_ref[slot, c] = s
            mx_ref[slot, c] = _lane_tile_max(s)

    def consume(j, slot):
        v = v_ref[0, key_slice(j), :]
        for c in range(2):
            _softmax_step(s_ref[slot, c], v, m_ref, l_ref, acc_ref, c, tile_max=mx_ref[slot, c])

    n_pairs = jnp.maximum(n_full - 1, 0) // 2

    @pl.when(n_full > 0)
    def _():
        produce(0, 0)

    def pair_body(t, carry):
        produce(2 * t + 1, 1)
        consume(2 * t, 0)
        produce(2 * t + 2, 0)
        consume(2 * t + 1, 1)
        return carry

    lax.fori_loop(0, n_pairs, pair_body, 0)
    left = n_full - 2 * n_pairs

    @pl.when(left == 2)
    def _():
        produce(2 * n_pairs + 1, 1)
        consume(2 * n_pairs, 0)
        consume(2 * n_pairs + 1, 1)

    @pl.when(left == 1)
    def _():
        consume(2 * n_pairs, 0)

    def tile_masked(j, carry):
        v = v_ref[0, key_slice(j), :]
        ok = _allowed(j, p0, tq, tk, n_keys)
        for c in range(2):
            _softmax_step(jnp.where(ok, scores(j, c), NEG), v, m_ref, l_ref, acc_ref, c)
        return carry

    lax.fori_loop(n_full, n_tot, tile_masked, 0)

    lam = (jnp.exp(jnp.sum(lq1_ref[...] * lk1_ref[...], axis=-1, keepdims=True))
           - jnp.exp(jnp.sum(lq2_ref[...] * lk2_ref[...], axis=-1, keepdims=True)) + lam_init)
    reps = db // LANES
    o = acc_ref[0] / _lane_tile(l_ref[0], reps) - lam * (acc_ref[1] / _lane_tile(l_ref[1], reps))
    ms = jnp.mean(o * o, axis=-1, keepdims=True)
    o_ref[0] = (o * lax.rsqrt(ms + EPS) * sg_ref[...] * (1.0 - lam_init)).astype(o_ref.dtype)


def _diff(qb, kb, vb, lams, subln_g, n_keys, pos_off, hd, lam_init):
    B, S, wb = qb.shape
    lp = kb.shape[1]
    db = 2 * hd
    hb = wb // db
    tq = _pick(S, (256, 128, 64, 32, 16, 8))
    tk = _key_tile(lp)
    kern = functools.partial(_diff_kernel, tq=tq, tk=tk, n_keys=n_keys, pos_off=pos_off, hd=hd,
                             lam_init=lam_init)
    vec = pl.BlockSpec((1, hd), lambda b, h, i: (0, 0))
    return pl.pallas_call(
        kern,
        grid=(B, hb, S // tq),
        in_specs=[pl.BlockSpec((1, tq, db), lambda b, h, i: (b, i, h)),
                  pl.BlockSpec((1, lp, db), lambda b, h, i: (b, 0, h)),
                  pl.BlockSpec((1, lp, db), lambda b, h, i: (b, 0, h)),
                  vec, vec, vec, vec,
                  pl.BlockSpec((1, db), lambda b, h, i: (0, 0))],
        out_specs=pl.BlockSpec((1, tq, db), lambda b, h, i: (b, i, h)),
        out_shape=jax.ShapeDtypeStruct((B, S, wb), BF16),
        scratch_shapes=[pltpu.VMEM((2, tq, LANES), F32),
                        pltpu.VMEM((2, tq, LANES), F32),
                        pltpu.VMEM((2, tq, db), F32),
                        pltpu.VMEM((2, 2, tq, tk), F32),
                        pltpu.VMEM((2, 2, tq, LANES), F32)],
        compiler_params=_cparams(("parallel", "parallel", "arbitrary")),
        name="diff_attention",
    )(qb, kb, vb, *lams, subln_g)


def _pad_rows(a, lp):
    return jnp.pad(a, ((0, 0), (0, lp - a.shape[1]), (0, 0)))


def _prep_w_in(w_in, lay, d_idx_w):
    ha, kv, hb, hd, di = lay.ha, lay.kv, lay.hb, lay.hd, lay.di
    widths = (ha * hd, kv * hd, kv * hd, N_IDX_HEADS * di, di, d_idx_w, hb * 2 * hd, hb * 2 * hd, hb * 2 * hd)
    offs = np.cumsum((0,) + widths)
    seg = [w_in[:, offs[i]:offs[i + 1]] for i in range(len(widths))]
    d = w_in.shape[0]
    pad_kiw = jnp.zeros((d, LANES - di - d_idx_w), w_in.dtype)
    pad_end = jnp.zeros((d, lay.width - lay.end), w_in.dtype)
    return jnp.concatenate(seg[:6] + [pad_kiw] + seg[6:] + [pad_end], axis=1).astype(BF16)


def _layer(x, B, S, mods, past, pos_off, lw, li):
    (sh1, sc1, gt1, sh2, sc2, gt2, sh3, sc3, gt3) = mods
    D = x.shape[1]
    lay = lw["lay"]
    hd, di, kv = lay.hd, lay.di, lay.kv

    h = _norm_mod(x, lw["norm_ffn1"], sc1, sh1)
    a = _mm_up(h, lw["w1_1"], lw["w3_1"])
    x = _mm_res([a], [lw["w2_1"]], x, gt1, 0.5, "ffn_down")

    h = _norm_mod(x, lw["norm_attn"], sc2, sh2)
    z = _mm_plain(h, lw["w_in"]).reshape(B, S, lay.width)
    (k_a, v_a, k_i, k_b, v_b, qa16, ka16, va16, qi16, kiw, kiw16, qb16, kb16, vb16) = _post(
        z, lay, lw["gains"], lw["tables"], pos_off)

    if past is not None:
        ck_a, cv_a, ck_i, ck_b, cv_b = past
        P = ck_a.shape[1]
        flat = lambda c: c.reshape(B, P, -1).astype(BF16)
        ka16 = jnp.concatenate([flat(ck_a), ka16], axis=1)
        va16 = jnp.concatenate([flat(cv_a), va16], axis=1)
        ck_i16 = jnp.pad(flat(ck_i), ((0, 0), (0, 0), (0, LANES - di)))
        kiw16_k = jnp.concatenate([ck_i16, kiw16], axis=1)
        kb16 = jnp.concatenate([flat(ck_b), kb16], axis=1)
        vb16 = jnp.concatenate([flat(cv_b), vb16], axis=1)
        n_keys = P + S
    else:
        kiw16_k = kiw16
        n_keys = S
    lp = _round_up(n_keys, 256)
    if lp != n_keys:
        ka16, va16, kiw16_k, kb16, vb16 = (_pad_rows(t, lp) for t in (ka16, va16, kiw16_k, kb16, vb16))

    oa = _dsa(qa16, qi16, kiw, ka16, va16, kiw16_k, n_keys, pos_off, kv, hd, di)
    ob = _diff(qb16, kb16, vb16, lw["lams"], lw["subln_g"], n_keys, pos_off, hd, _lambda_init(li))
    wa = oa.shape[2]
    x = _mm_res([oa.reshape(B * S, wa), ob.reshape(B * S, -1)], [lw["w_out"][:wa], lw["w_out"][wa:]],
                x, gt2, 1.0, "out_proj")

    h = _norm_mod(x, lw["norm_ffn2"], sc3, sh3)
    a = _mm_up(h, lw["w1_2"], lw["w3_2"])
    x = _mm_res([a], [lw["w2_2"]], x, gt3, 0.5, "ffn_down")
    return x, (k_a, v_a, k_i, k_b, v_b)


def kernel(x_prompt, x_sample, cache_k_a, cache_v_a, cache_k_idx, cache_k_b, cache_v_b, c_prompt, c_sample, w_ada, b_ada, norm_ffn1, w1_ffn1, w3_ffn1, w2_ffn1, norm_attn, w_in, qnorm_a, knorm_a, knorm_idx, qnorm_b, knorm_b, lam_q1, lam_k1, lam_q2, lam_k2, subln_b, w_out, norm_ffn2, w1_ffn2, w3_ffn2, w2_ffn2):
    depth = w_ada.shape[0]
    Bp, Sp, D = x_prompt.shape
    Bs, Ss, _ = x_sample.shape
    P = cache_k_a.shape[2]
    kv, hd = cache_k_a.shape[3], cache_k_a.shape[4]
    di = cache_k_idx.shape[3]
    hb = cache_k_b.shape[3]
    db = cache_v_b.shape[4]
    ha = (w_out.shape[1] - hb * db) // hd
    lay = _Layout(ha, kv, hb, hd, di)
    tables = _rope_tables(hd, di)

    xp = x_prompt.reshape(Bp * Sp, D)
    xs = x_sample.reshape(Bs * Ss, D)
    n_c = Bp + Bs
    c_all = jnp.pad(jnp.concatenate([c_prompt, c_sample], axis=0), ((0, _round_up(n_c, 8) - n_c), (0, 0)))

    rows_p, rows_s = [], []
    for l in range(depth):
        row = lambda a: a[l][None, :]
        gki = jnp.pad(knorm_idx[l], (0, LANES - di))[None, :]
        lw = dict(
            lay=lay, tables=tables,
            norm_ffn1=row(norm_ffn1), norm_attn=row(norm_attn), norm_ffn2=row(norm_ffn2),
            w1_1=w1_ffn1[l], w3_1=w3_ffn1[l], w2_1=w2_ffn1[l].astype(BF16),
            w1_2=w1_ffn2[l], w3_2=w3_ffn2[l], w2_2=w2_ffn2[l].astype(BF16),
            w_in=_prep_w_in(w_in[l], lay, N_IDX_HEADS), w_out=w_out[l].astype(BF16),
            gains=(row(qnorm_a), row(knorm_a), gki, row(qnorm_b), row(knorm_b)),
            lams=(row(lam_q1), row(lam_k1), row(lam_q2), row(lam_k2)),
            subln_g=row(subln_b),
        )
        mod = _ada(c_all, w_ada[l], b_ada[l][None, :])

        def mods_for(lo, nb, s):
            m = mod[lo:lo + nb].reshape(nb, N_MOD, D)
            if nb == 1:
                return tuple(m[:, i] for i in range(N_MOD))
            return tuple(jnp.repeat(m[:, i], s, axis=0) for i in range(N_MOD))

        xp, rp = _layer(xp, Bp, Sp, mods_for(0, Bp, Sp), None, 0, lw, l)
        past = (cache_k_a[l], cache_v_a[l], cache_k_idx[l], cache_k_b[l], cache_v_b[l])
        xs, rs = _layer(xs, Bs, Ss, mods_for(Bp, Bs, Ss), past, P, lw, l)
        rows_p.append(rp)
        rows_s.append(rs)

    def stack(rows, idx, B, S, tail):
        return jnp.stack([r[idx].reshape((B, S) + tail) for r in rows])

    tails = ((kv, hd), (kv, hd), (di,), (hb, 2, hd), (hb, db))
    outs_p = tuple(stack(rows_p, i, Bp, Sp, tails[i]) for i in range(5))
    outs_s = tuple(stack(rows_s, i, Bs, Ss, tails[i]) for i in range(5))
    return (xp.reshape(Bp, Sp, D), xs.reshape(Bs, Ss, D)) + outs_p + outs_s
```

```python
import functools
import math

import numpy as np
import jax
import jax.numpy as jnp
from jax import lax
from jax.experimental import pallas as pl
from jax.experimental.pallas import tpu as pltpu

F32 = jnp.float32
BF16 = jnp.bfloat16

CHUNK = 64
N_IDX_HEADS = 16
IDX_STACK = 4
TOPK_MAX = 256
ROPE_THETA = 500000.0
ROT_FRAC = 4
EPS = 1e-6
N_MOD = 9

LANES = 128
V7X_VMEM_BYTES = 64 * 1024 * 1024
VMEM_LIMIT = V7X_VMEM_BYTES * 7 // 8
NEG = -1e30
NT_DIMS = (((1,), (1,)), ((), ()))


def _lambda_init(layer_idx):
    return 0.8 - 0.6 * math.exp(-0.3 * layer_idx)


def _pick(n, candidates):
    for c in candidates:
        if n % c == 0:
            return c
    return n


def _cparams(semantics):
    return pltpu.CompilerParams(dimension_semantics=semantics, vmem_limit_bytes=VMEM_LIMIT)


def _round_up(n, m):
    return (n + m - 1) // m * m


def _ada_kernel(c_ref, w_ref, b_ref, o_ref):
    c = c_ref[...]
    s = (c * jax.nn.sigmoid(c)).astype(BF16)
    o_ref[...] = jnp.dot(s, w_ref[...].astype(BF16), preferred_element_type=F32) + b_ref[...]


def _ada(c, w, b):
    R, D = c.shape
    N = w.shape[1]
    tn = _pick(N, (512, 256, 128))
    return pl.pallas_call(
        _ada_kernel,
        grid=(N // tn,),
        in_specs=[pl.BlockSpec((R, D), lambda j: (0, 0)),
                  pl.BlockSpec((D, tn), lambda j: (0, j)),
                  pl.BlockSpec((1, tn), lambda j: (0, j))],
        out_specs=pl.BlockSpec((R, tn), lambda j: (0, j)),
        out_shape=jax.ShapeDtypeStruct((R, N), F32),
        compiler_params=_cparams(("parallel",)),
        name="ada_matvec",
    )(c, w, b)


def _norm_mod_kernel(x_ref, g_ref, sc_ref, sh_ref, o_ref):
    x = x_ref[...]
    ms = jnp.mean(x * x, axis=-1, keepdims=True)
    y = x * lax.rsqrt(ms + EPS) * g_ref[...]
    o_ref[...] = (y * (1.0 + sc_ref[...]) + sh_ref[...]).astype(o_ref.dtype)


def _norm_mod(x, g, scale, shift):
    M, D = x.shape
    tm = _pick(M, (256, 128, 64, 32, 16, 8))

    def mspec(a):
        if a.shape[0] == 1:
            return pl.BlockSpec((1, D), lambda i: (0, 0))
        return pl.BlockSpec((tm, D), lambda i: (i, 0))

    return pl.pallas_call(
        _norm_mod_kernel,
        grid=(M // tm,),
        in_specs=[pl.BlockSpec((tm, D), lambda i: (i, 0)),
                  pl.BlockSpec((1, D), lambda i: (0, 0)),
                  mspec(scale), mspec(shift)],
        out_specs=pl.BlockSpec((tm, D), lambda i: (i, 0)),
        out_shape=jax.ShapeDtypeStruct((M, D), BF16),
        compiler_params=_cparams(("parallel",)),
        name="norm_mod",
    )(x, g, scale, shift)


def _mm_up_kernel(h_ref, w1_ref, w3_ref, o_ref):
    h = h_ref[...]
    u = jnp.dot(h, w1_ref[...].astype(BF16), preferred_element_type=F32)
    v = jnp.dot(h, w3_ref[...].astype(BF16), preferred_element_type=F32)
    o_ref[...] = (u * jax.nn.sigmoid(u) * v).astype(o_ref.dtype)


def _mm_up(h, w1, w3):
    M, K = h.shape
    F = w1.shape[1]
    tm = _pick(M, (1024, 512, 256, 128, 64, 32, 16, 8))
    tn = _pick(F, (256, 128))
    return pl.pallas_call(
        _mm_up_kernel,
        grid=(M // tm, F // tn),
        in_specs=[pl.BlockSpec((tm, K), lambda i, j: (i, 0)),
                  pl.BlockSpec((K, tn), lambda i, j: (0, j)),
                  pl.BlockSpec((K, tn), lambda i, j: (0, j))],
        out_specs=pl.BlockSpec((tm, tn), lambda i, j: (i, j)),
        out_shape=jax.ShapeDtypeStruct((M, F), BF16),
        compiler_params=_cparams(("parallel", "parallel")),
        name="ffn_up",
    )(h, w1, w3)


def _mm_plain_kernel(h_ref, w_ref, o_ref):
    o_ref[...] = jnp.dot(h_ref[...], w_ref[...], preferred_element_type=F32)


def _mm_plain(h, w):
    M, K = h.shape
    N = w.shape[1]
    tm = _pick(M, (1024, 512, 256, 128, 64, 32, 16, 8))
    tn = _pick(N, (512, 256, 128))
    return pl.pallas_call(
        _mm_plain_kernel,
        grid=(M // tm, N // tn),
        in_specs=[pl.BlockSpec((tm, K), lambda i, j: (i, 0)),
                  pl.BlockSpec((K, tn), lambda i, j: (0, j))],
        out_specs=pl.BlockSpec((tm, tn), lambda i, j: (i, j)),
        out_shape=jax.ShapeDtypeStruct((M, N), F32),
        compiler_params=_cparams(("parallel", "parallel")),
        name="in_proj",
    )(h, w)


def _mm_res_kernel(*refs, n_lhs, coef):
    a_refs = refs[:n_lhs]
    w_refs = refs[n_lhs:2 * n_lhs]
    x_ref, g_ref, o_ref = refs[2 * n_lhs:]
    y = jnp.dot(a_refs[0][...], w_refs[0][...], preferred_element_type=F32)
    for a_ref, w_ref in zip(a_refs[1:], w_refs[1:]):
        y = y + jnp.dot(a_ref[...], w_ref[...], preferred_element_type=F32)
    o_ref[...] = x_ref[...] + (coef * g_ref[...]) * y


def _mm_res(a_list, w_list, x, gate, coef, name):
    M, N = x.shape
    ktot = sum(a.shape[1] for a in a_list)
    tm = _pick(M, (1024, 512, 256, 128, 64, 32, 16, 8))
    tn = _pick(N, (512, 256, 128)) if ktot <= 4096 else _pick(N, (256, 128))
    lhs_mode = {} if ktot <= 4096 else dict(pipeline_mode=pl.Buffered(1))
    in_specs = [pl.BlockSpec((tm, a.shape[1]), lambda i, j: (i, 0), **lhs_mode) for a in a_list]
    in_specs += [pl.BlockSpec((w.shape[0], tn), lambda i, j: (0, j)) for w in w_list]
    in_specs.append(pl.BlockSpec((tm, tn), lambda i, j: (i, j)))
    if gate.shape[0] == 1:
        in_specs.append(pl.BlockSpec((1, tn), lambda i, j: (0, j)))
    else:
        in_specs.append(pl.BlockSpec((tm, tn), lambda i, j: (i, j)))
    return pl.pallas_call(
        functools.partial(_mm_res_kernel, n_lhs=len(a_list), coef=coef),
        grid=(M // tm, N // tn),
        in_specs=in_specs,
        out_specs=pl.BlockSpec((tm, tn), lambda i, j: (i, j)),
        out_shape=jax.ShapeDtypeStruct((M, N), F32),
        compiler_params=_cparams(("parallel", "parallel")),
        name=name,
    )(*a_list, *w_list, x, gate)


class _Layout:
    def __init__(self, ha, kv, hb, hd, di):
        self.ha, self.kv, self.hb, self.hd, self.di = ha, kv, hb, hd, di
        self.q_a = 0
        self.k_a = self.q_a + ha * hd
        self.v_a = self.k_a + kv * hd
        self.q_i = self.v_a + kv * hd
        self.kiw = self.q_i + N_IDX_HEADS * di
        self.q_b = self.kiw + LANES
        self.k_b = self.q_b + hb * 2 * hd
        self.v_b = self.k_b + hb * 2 * hd
        self.end = self.v_b + hb * 2 * hd
        self.width = _round_up(self.end, 512)


def _rope_tables(hd, di):
    def table(d):
        r = d // ROT_FRAC
        half = r // 2
        inv = ROPE_THETA ** (-(jnp.arange(half, dtype=jnp.float32) * 2.0 / r))
        lane = np.arange(LANES) % d
        idx = np.where(lane < half, lane, np.where(lane < r, lane - half, 0))
        invl = jnp.where(lane < r, inv[idx], 0.0)
        sign = np.where(lane < half, -1.0, np.where(lane < r, 1.0, 0.0)).astype(np.float32)
        src = np.where(lane < half, np.arange(LANES) + half, np.arange(LANES) - half)
        partner = np.zeros((LANES, LANES), np.float32)
        partner[src[lane < r], np.arange(LANES)[lane < r]] = 1.0
        return jnp.stack([invl, jnp.asarray(sign)])[:, None, :], jnp.asarray(partner, dtype=BF16)
    th, ph = table(hd)
    ti, pi = table(di)
    return th, ph, ti, pi


def _split3(x):
    x1 = x.astype(BF16)
    r1 = x - x1.astype(F32)
    x2 = r1.astype(BF16)
    x3 = (r1 - x2.astype(F32)).astype(BF16)
    return x1, x2, x3


def _post_kernel(z_ref, gqa_ref, gka_ref, gki_ref, gqb_ref, gkb_ref, th_ref, ph_ref, ti_ref, pi_ref,
                 ka_ref, va_ref, ki_ref, kb_ref, vb_ref,
                 qa16_ref, ka16_ref, va16_ref, qi16_ref, kiw_ref, kiw16_ref, qb16_ref, kb16_ref, vb16_ref,
                 *, lay, ts, pos_off, scale):
    hd, di = lay.hd, lay.di
    row = lax.broadcasted_iota(jnp.int32, (ts, LANES), 0)
    pos = (pos_off + pl.program_id(1) * ts + row).astype(F32)

    def rope_fn(t_ref, p_ref):
        ang = pos * t_ref[0]
        cosv = jnp.cos(ang)
        sinv = jnp.sin(ang) * t_ref[1]

        def rope(x):
            partner = sum(jnp.dot(t, p_ref[...], preferred_element_type=F32) for t in _split3(x))
            return x * cosv + partner * sinv
        return rope

    rope_h = rope_fn(th_ref, ph_ref)
    rope_i = rope_fn(ti_ref, pi_ref)

    def hnorm(x, g):
        ms = jnp.mean(x * x, axis=-1, keepdims=True)
        return x * lax.rsqrt(ms + EPS) * g

    def slab(off, h):
        return z_ref[0, :, off + h * hd: off + (h + 1) * hd]

    def cols(h):
        return slice(h * hd, (h + 1) * hd)

    for h in range(lay.ha):
        q = rope_h(hnorm(slab(lay.q_a, h), gqa_ref[...]))
        qa16_ref[0, :, cols(h)] = (q * scale).astype(BF16)
    for h in range(lay.kv):
        k = rope_h(hnorm(slab(lay.k_a, h), gka_ref[...]))
        ka_ref[0, :, cols(h)] = k
        ka16_ref[0, :, cols(h)] = k.astype(BF16)
        v = slab(lay.v_a, h)
        va_ref[0, :, cols(h)] = v
        va16_ref[0, :, cols(h)] = v.astype(BF16)
    for h in range(N_IDX_HEADS * di // LANES):
        qi16_ref[0, :, cols(h)] = rope_i(slab(lay.q_i, h)).astype(BF16)
    x = slab(lay.kiw, 0)
    lane = lax.broadcasted_iota(jnp.int32, (ts, LANES), 1)
    is_k = lane < di
    ms = jnp.sum(jnp.where(is_k, x * x, 0.0), axis=-1, keepdims=True) * (1.0 / di)
    ki = rope_i(x * lax.rsqrt(ms + EPS) * gki_ref[...])
    ki_ref[0] = ki[:, :di]
    kiw = jnp.where(is_k, ki, x)
    kiw_ref[0] = kiw
    kiw16_ref[0] = kiw.astype(BF16)
    for h in range(2 * lay.hb):
        q = rope_h(hnorm(slab(lay.q_b, h), gqb_ref[...]))
        qb16_ref[0, :, cols(h)] = (q * scale).astype(BF16)
        k = rope_h(hnorm(slab(lay.k_b, h), gkb_ref[...]))
        kb_ref[0, :, cols(h)] = k
        kb16_ref[0, :, cols(h)] = k.astype(BF16)
        v = slab(lay.v_b, h)
        vb_ref[0, :, cols(h)] = v
        vb16_ref[0, :, cols(h)] = v.astype(BF16)


def _post(z, lay, gains, tables, pos_off):
    B, S, W = z.shape
    hd, di = lay.hd, lay.di
    ts = _pick(S, (128, 64, 32, 16, 8))
    wa, wkv, wqi, wb = lay.ha * hd, lay.kv * hd, N_IDX_HEADS * di, lay.hb * 2 * hd

    def rows(width):
        return pl.BlockSpec((1, ts, width), lambda b, s: (b, s, 0))

    def const(a):
        return pl.BlockSpec(a.shape, lambda b, s: (0,) * a.ndim)

    out_widths = [(wkv, F32), (wkv, F32), (di, F32), (wb, F32), (wb, F32),
                  (wa, BF16), (wkv, BF16), (wkv, BF16), (wqi, BF16), (LANES, F32), (LANES, BF16),
                  (wb, BF16), (wb, BF16), (wb, BF16)]
    return pl.pallas_call(
        functools.partial(_post_kernel, lay=lay, ts=ts, pos_off=pos_off,
                          scale=hd ** -0.5 * math.log2(math.e)),
        grid=(B, S // ts),
        in_specs=[rows(W)] + [const(g) for g in gains] + [const(t) for t in tables],
        out_specs=[rows(w) for w, _ in out_widths],
        out_shape=[jax.ShapeDtypeStruct((B, S, w), dt) for w, dt in out_widths],
        compiler_params=_cparams(("parallel", "parallel")),
        name="head_norm_rope",
    )(z, *gains, *tables)


def _tile_bounds(p0, tq, tk, n_keys):
    a_full = jnp.minimum((p0 // CHUNK + 1) * CHUNK, n_keys)
    a_tot = jnp.minimum(((p0 + tq - 1) // CHUNK + 1) * CHUNK, n_keys)
    return a_full // tk, (a_tot + tk - 1) // tk


def _allowed(j, p0, tq, tk, n_keys):
    shift = CHUNK.bit_length() - 1
    kpos = j * tk + lax.broadcasted_iota(jnp.int32, (tq, tk), 1)
    qpos = p0 + lax.broadcasted_iota(jnp.int32, (tq, tk), 0)
    return (lax.shift_right_logical(kpos, shift) <= lax.shift_right_logical(qpos, shift)) & (kpos < n_keys)


def _lane_tile_list(parts):
    return parts[0] if len(parts) == 1 else jnp.concatenate(parts, axis=1)


def _lane_tile(x, reps):
    return _lane_tile_list([x] * reps)


def _key_tile(lp):
    return 1024 if lp % 1024 == 0 else 256


def _lane_tile_max(s):
    out = s[:, :LANES]
    for c in range(1, s.shape[1] // LANES):
        out = jnp.maximum(out, s[:, c * LANES:(c + 1) * LANES])
    return out


def _softmax_step(s, v, m_ref, l_ref, acc_ref, idx, tile_max=None):
    tk = s.shape[1]
    wv = acc_ref.shape[-1]
    m_old = m_ref[idx]
    if tile_max is None:
        tile_max = _lane_tile_max(s)
    m_new = jnp.maximum(m_old, jnp.max(tile_max, axis=1, keepdims=True))
    alpha = jnp.exp2(m_old - m_new)
    p = jnp.exp2(s - _lane_tile(m_new, tk // LANES))
    if l_ref is not None:
        l_ref[idx] = alpha * l_ref[idx] + jnp.sum(p, axis=1, keepdims=True)
    acc_ref[idx] = _lane_tile(alpha, wv // LANES) * acc_ref[idx] + jnp.dot(
        p.astype(BF16), v, preferred_element_type=F32)
    m_ref[idx] = m_new


def _dsa_kernel(qa_ref, qi_ref, wq_ref, ka_ref, va_ref, ki_ref, o_ref,
                keys_ref, wexp_ref, qih_ref, tau_ref, bias_ref, m_ref, acc_ref, s_ref, mx_ref,
                *, tq, tk, n_keys, pos_off, topk, ha, kv, hd, di):
    p0 = pos_off + pl.program_id(1) * tq
    n_full, n_tot = _tile_bounds(p0, tq, tk, n_keys)
    reps = tk // LANES
    group = ha // kv
    int_min = jnp.int32(-2 ** 31)

    def rows(r):
        return slice(r * tq, (r + 1) * tq)

    lane = lax.broadcasted_iota(jnp.int32, (tq, LANES), 1)
    w_scale = (N_IDX_HEADS ** -0.5) * (di ** -0.5)
    for h in range(N_IDX_HEADS):
        pair = qi_ref[0, :, (h * di // LANES) * LANES:(h * di // LANES + 1) * LANES].astype(F32)
        off = (h * di) % LANES
        if off:
            pair = pltpu.roll(pair, LANES - off, 1)
        qih_ref[h // IDX_STACK, rows(h % IDX_STACK), :] = jnp.where(lane < di, pair, 0.0).astype(BF16)
        wexp_ref[h] = jnp.broadcast_to(wq_ref[0, :, di + h:di + h + 1] * w_scale, (tq, LANES))

    def key_slice(j):
        return pl.ds(pl.multiple_of(j * tk, tk), tk)

    def score_tile(j, masked):
        kt = ki_ref[0, key_slice(j), :]
        parts = [jnp.zeros((tq, LANES), F32)] * reps
        for hs in range(N_IDX_HEADS // IDX_STACK):
            lg = lax.dot_general(qih_ref[hs], kt, NT_DIMS, preferred_element_type=F32)
            for r in range(IDX_STACK):
                w = wexp_ref[hs * IDX_STACK + r]
                parts = [parts[c] + jnp.maximum(lg[rows(r), c * LANES:(c + 1) * LANES], 0.0) * w
                         for c in range(reps)]
        sc = _lane_tile_list(parts)
        if masked:
            sc = jnp.where(_allowed(j, p0, tq, tk, n_keys), sc, -jnp.inf)
        bits = pltpu.bitcast(sc, jnp.int32)
        keys_ref[j] = jnp.where(bits < 0, bits ^ jnp.int32(0x7FFFFFFF), bits)

    def score_full(j, c):
        score_tile(j, False)
        return c

    def score_masked(j, c):
        score_tile(j, True)
        return c

    lax.fori_loop(0, n_full, score_full, 0)
    lax.fori_loop(n_full, n_tot, score_masked, 0)

    def count_ge(cand):
        def body(j, acc):
            t = keys_ref[j]
            for c in range(reps):
                acc = jnp.where(t[:, c * LANES:(c + 1) * LANES] >= cand, acc + 1.0, acc)
            return acc
        acc = lax.fori_loop(0, n_tot, body, jnp.zeros((tq, LANES), F32))
        return jnp.sum(acc, axis=1, keepdims=True)

    kf = float(topk)
    zero = jnp.zeros((tq, LANES), jnp.int32)
    prefix = jnp.where(count_ge(zero) >= kf, zero, int_min)

    def bit_body(b, prefix):
        cand = prefix + lax.shift_left(jnp.int32(1), 30 - b)
        return jnp.where(count_ge(cand) >= kf, cand, prefix)

    prefix = lax.fori_loop(0, 31, bit_body, prefix)
    tau_ref[...] = prefix

    m_ref[...] = jnp.full(m_ref.shape, NEG, F32)
    acc_ref[...] = jnp.zeros(acc_ref.shape, F32)
    ones = jnp.ones((tk, hd), BF16)

    def attn_tile(j, masked):
        sel = keys_ref[j] >= _lane_tile(tau_ref[...], reps)
        if masked:
            sel = sel & _allowed(j, p0, tq, tk, n_keys)
        bias_ref[...] = jnp.where(sel, 0.0, NEG)
        def produce(h):
            k = ka_ref[0, key_slice(j), (h // group) * hd:(h // group + 1) * hd]
            s = lax.dot_general(qa_ref[0, :, h * hd:(h + 1) * hd], k, NT_DIMS, preferred_element_type=F32)
            s = s + bias_ref[...]
            s_ref[h % 2] = s
            mx_ref[h % 2] = _lane_tile_max(s)

        def consume(h):
            g = h // group
            v1 = jnp.concatenate([va_ref[0, key_slice(j), g * hd:(g + 1) * hd], ones], axis=1)
            _softmax_step(s_ref[h % 2], v1, m_ref, None, acc_ref, h, tile_max=mx_ref[h % 2])

        produce(0)
        for h in range(ha):
            if h + 1 < ha:
                produce(h + 1)
            consume(h)

    def attn_full(j, c):
        attn_tile(j, False)
        return c

    def attn_masked(j, c):
        attn_tile(j, True)
        return c

    lax.fori_loop(0, n_full, attn_full, 0)
    lax.fori_loop(n_full, n_tot, attn_masked, 0)
    for h in range(ha):
        o_ref[0, :, h * hd:(h + 1) * hd] = (acc_ref[h, :, :hd] / acc_ref[h, :, hd:]).astype(o_ref.dtype)


def _dsa(qa, qi, kiw_q, ka, va, kiw_k, n_keys, pos_off, kv, hd, di):
    B, S, wa = qa.shape
    lp = ka.shape[1]
    ha = wa // hd
    tq = _pick(S, (128, 64, 32, 16, 8))
    tk = _key_tile(lp)
    topk = min(TOPK_MAX, n_keys // 4)
    kern = functools.partial(_dsa_kernel, tq=tq, tk=tk, n_keys=n_keys, pos_off=pos_off, topk=topk,
                             ha=ha, kv=kv, hd=hd, di=di)

    def rows(width):
        return pl.BlockSpec((1, tq, width), lambda b, i: (b, i, 0))

    def resident(width):
        return pl.BlockSpec((1, lp, width), lambda b, i: (b, 0, 0), pipeline_mode=pl.Buffered(1))

    return pl.pallas_call(
        kern,
        grid=(B, S // tq),
        in_specs=[rows(wa), rows(qi.shape[2]), rows(LANES),
                  resident(kv * hd), resident(kv * hd), resident(LANES)],
        out_specs=rows(wa),
        out_shape=jax.ShapeDtypeStruct((B, S, wa), BF16),
        scratch_shapes=[pltpu.VMEM((lp // tk, tq, tk), jnp.int32),
                        pltpu.VMEM((N_IDX_HEADS, tq, LANES), F32),
                        pltpu.VMEM((N_IDX_HEADS // IDX_STACK, IDX_STACK * tq, LANES), BF16),
                        pltpu.VMEM((tq, LANES), jnp.int32),
                        pltpu.VMEM((tq, tk), F32),
                        pltpu.VMEM((ha, tq, LANES), F32),
                        pltpu.VMEM((ha, tq, 2 * hd), F32),
                        pltpu.VMEM((2, tq, tk), F32),
                        pltpu.VMEM((2, tq, LANES), F32)],
        compiler_params=_cparams(("parallel", "arbitrary")),
        name="dsa_attention",
    )(qa, qi, kiw_q, ka, va, kiw_k)


def _diff_kernel(q_ref, k_ref, v_ref, lq1_ref, lk1_ref, lq2_ref, lk2_ref, sg_ref, o_ref,
                 m_ref, l_ref, acc_ref, s_ref, mx_ref, *, tq, tk, n_keys, pos_off, hd, lam_init):
    p0 = pos_off + pl.program_id(2) * tq
    n_full, n_tot = _tile_bounds(p0, tq, tk, n_keys)
    db = 2 * hd
    m_ref[...] = jnp.full(m_ref.shape, NEG, F32)
    l_ref[...] = jnp.zeros(l_ref.shape, F32)
    acc_ref[...] = jnp.zeros(acc_ref.shape, F32)

    def key_slice(j):
        return pl.ds(pl.multiple_of(j * tk, tk), tk)

    def scores(j, c):
        return lax.dot_general(q_ref[0, :, c * hd:(c + 1) * hd], k_ref[0, key_slice(j), c * hd:(c + 1) * hd],
                               NT_DIMS, preferred_element_type=F32)

    def produce(j, slot):
        for c in range(2):
            s = scores(j, c)
            s_ref[slot, c] = s
            mx_ref[slot, c] = _lane_tile_max(s)

    def consume(j, slot):
        v = v_ref[0, key_slice(j), :]
        for c in range(2):
            _softmax_step(s_ref[slot, c], v, m_ref, l_ref, acc_ref, c, tile_max=mx_ref[slot, c])

    n_pairs = jnp.maximum(n_full - 1, 0) // 2

    @pl.when(n_full > 0)
    def _():
        produce(0, 0)

    def pair_body(t, carry):
        produce(2 * t + 1, 1)
        consume(2 * t, 0)
        produce(2 * t + 2, 0)
        consume(2 * t + 1, 1)
        return carry

    lax.fori_loop(0, n_pairs, pair_body, 0)
    left = n_full - 2 * n_pairs

    @pl.when(left == 2)
    def _():
        produce(2 * n_pairs + 1, 1)
        consume(2 * n_pairs, 0)
        consume(2 * n_pairs + 1, 1)

    @pl.when(left == 1)
    def _():
        consume(2 * n_pairs, 0)

    def tile_masked(j, carry):
        v = v_ref[0, key_slice(j), :]
        ok = _allowed(j, p0, tq, tk, n_keys)
        for c in range(2):
            _softmax_step(jnp.where(ok, scores(j, c), NEG), v, m_ref, l_ref, acc_ref, c)
        return carry

    lax.fori_loop(n_full, n_tot, tile_masked, 0)

    lam = (jnp.exp(jnp.sum(lq1_ref[...] * lk1_ref[...], axis=-1, keepdims=True))
           - jnp.exp(jnp.sum(lq2_ref[...] * lk2_ref[...], axis=-1, keepdims=True)) + lam_init)
    reps = db // LANES
    o = acc_ref[0] / _lane_tile(l_ref[0], reps) - lam * (acc_ref[1] / _lane_tile(l_ref[1], reps))
    ms = jnp.mean(o * o, axis=-1, keepdims=True)
    o_ref[0] = (o * lax.rsqrt(ms + EPS) * sg_ref[...] * (1.0 - lam_init)).astype(o_ref.dtype)


def _diff(qb, kb, vb, lams, subln_g, n_keys, pos_off, hd, lam_init):
    B, S, wb = qb.shape
    lp = kb.shape[1]
    db = 2 * hd
    hb = wb // db
    tq = _pick(S, (512, 256, 128, 64, 32, 16, 8))
    tk = _key_tile(lp)
    kern = functools.partial(_diff_kernel, tq=tq, tk=tk, n_keys=n_keys, pos_off=pos_off, hd=hd,
                             lam_init=lam_init)
    vec = pl.BlockSpec((1, hd), lambda b, h, i: (0, 0))
    return pl.pallas_call(
        kern,
        grid=(B, hb, S // tq),
        in_specs=[pl.BlockSpec((1, tq, db), lambda b, h, i: (b, i, h)),
                  pl.BlockSpec((1, lp, db), lambda b, h, i: (b, 0, h)),
                  pl.BlockSpec((1, lp, db), lambda b, h, i: (b, 0, h)),
                  vec, vec, vec, vec,
                  pl.BlockSpec((1, db), lambda b, h, i: (0, 0))],
        out_specs=pl.BlockSpec((1, tq, db), lambda b, h, i: (b, i, h)),
        out_shape=jax.ShapeDtypeStruct((B, S, wb), BF16),
        scratch_shapes=[pltpu.VMEM((2, tq, LANES), F32),
                        pltpu.VMEM((2, tq, LANES), F32),
                        pltpu.VMEM((2, tq, db), F32),
                        pltpu.VMEM((2, 2, tq, tk), F32),
                        pltpu.VMEM((2, 2, tq, LANES), F32)],
        compiler_params=_cparams(("parallel", "parallel", "arbitrary")),
        name="diff_attention",
    )(qb, kb, vb, *lams, subln_g)


def _pad_rows(a, lp):
    return jnp.pad(a, ((0, 0), (0, lp - a.shape[1]), (0, 0)))


def _prep_w_in(w_in, lay, d_idx_w):
    ha, kv, hb, hd, di = lay.ha, lay.kv, lay.hb, lay.hd, lay.di
    widths = (ha * hd, kv * hd, kv * hd, N_IDX_HEADS * di, di, d_idx_w, hb * 2 * hd, hb * 2 * hd, hb * 2 * hd)
    offs = np.cumsum((0,) + widths)
    seg = [w_in[:, offs[i]:offs[i + 1]] for i in range(len(widths))]
    d = w_in.shape[0]
    pad_kiw = jnp.zeros((d, LANES - di - d_idx_w), w_in.dtype)
    pad_end = jnp.zeros((d, lay.width - lay.end), w_in.dtype)
    return jnp.concatenate(seg[:6] + [pad_kiw] + seg[6:] + [pad_end], axis=1).astype(BF16)


def _layer(x, B, S, mods, past, pos_off, lw, li):
    (sh1, sc1, gt1, sh2, sc2, gt2, sh3, sc3, gt3) = mods
    D = x.shape[1]
    lay = lw["lay"]
    hd, di, kv = lay.hd, lay.di, lay.kv

    h = _norm_mod(x, lw["norm_ffn1"], sc1, sh1)
    a = _mm_up(h, lw["w1_1"], lw["w3_1"])
    x = _mm_res([a], [lw["w2_1"]], x, gt1, 0.5, "ffn_down")

    h = _norm_mod(x, lw["norm_attn"], sc2, sh2)
    z = _mm_plain(h, lw["w_in"]).reshape(B, S, lay.width)
    (k_a, v_a, k_i, k_b, v_b, qa16, ka16, va16, qi16, kiw, kiw16, qb16, kb16, vb16) = _post(
        z, lay, lw["gains"], lw["tables"], pos_off)

    if past is not None:
        ck_a, cv_a, ck_i, ck_b, cv_b = past
        P = ck_a.shape[1]
        flat = lambda c: c.reshape(B, P, -1).astype(BF16)
        ka16 = jnp.concatenate([flat(ck_a), ka16], axis=1)
        va16 = jnp.concatenate([flat(cv_a), va16], axis=1)
        ck_i16 = jnp.pad(flat(ck_i), ((0, 0), (0, 0), (0, LANES - di)))
        kiw16_k = jnp.concatenate([ck_i16, kiw16], axis=1)
        kb16 = jnp.concatenate([flat(ck_b), kb16], axis=1)
        vb16 = jnp.concatenate([flat(cv_b), vb16], axis=1)
        n_keys = P + S
    else:
        kiw16_k = kiw16
        n_keys = S
    lp = _round_up(n_keys, 256)
    if lp != n_keys:
        ka16, va16, kiw16_k, kb16, vb16 = (_pad_rows(t, lp) for t in (ka16, va16, kiw16_k, kb16, vb16))

    oa = _dsa(qa16, qi16, kiw, ka16, va16, kiw16_k, n_keys, pos_off, kv, hd, di)
    ob = _diff(qb16, kb16, vb16, lw["lams"], lw["subln_g"], n_keys, pos_off, hd, _lambda_init(li))
    wa = oa.shape[2]
    x = _mm_res([oa.reshape(B * S, wa), ob.reshape(B * S, -1)], [lw["w_out"][:wa], lw["w_out"][wa:]],
                x, gt2, 1.0, "out_proj")

    h = _norm_mod(x, lw["norm_ffn2"], sc3, sh3)
    a = _mm_up(h, lw["w1_2"], lw["w3_2"])
    x = _mm_res([a], [lw["w2_2"]], x, gt3, 0.5, "ffn_down")
    return x, (k_a, v_a, k_i, k_b, v_b)


def kernel(x_prompt, x_sample, cache_k_a, cache_v_a, cache_k_idx, cache_k_b, cache_v_b, c_prompt, c_sample, w_ada, b_ada, norm_ffn1, w1_ffn1, w3_ffn1, w2_ffn1, norm_attn, w_in, qnorm_a, knorm_a, knorm_idx, qnorm_b, knorm_b, lam_q1, lam_k1, lam_q2, lam_k2, subln_b, w_out, norm_ffn2, w1_ffn2, w3_ffn2, w2_ffn2):
    depth = w_ada.shape[0]
    Bp, Sp, D = x_prompt.shape
    Bs, Ss, _ = x_sample.shape
    P = cache_k_a.shape[2]
    kv, hd = cache_k_a.shape[3], cache_k_a.shape[4]
    di = cache_k_idx.shape[3]
    hb = cache_k_b.shape[3]
    db = cache_v_b.shape[4]
    ha = (w_out.shape[1] - hb * db) // hd
    lay = _Layout(ha, kv, hb, hd, di)
    tables = _rope_tables(hd, di)

    xp = x_prompt.reshape(Bp * Sp, D)
    xs = x_sample.reshape(Bs * Ss, D)
    n_c = Bp + Bs
    c_all = jnp.pad(jnp.concatenate([c_prompt, c_sample], axis=0), ((0, _round_up(n_c, 8) - n_c), (0, 0)))

    rows_p, rows_s = [], []
    for l in range(depth):
        row = lambda a: a[l][None, :]
        gki = jnp.pad(knorm_idx[l], (0, LANES - di))[None, :]
        lw = dict(
            lay=lay, tables=tables,
            norm_ffn1=row(norm_ffn1), norm_attn=row(norm_attn), norm_ffn2=row(norm_ffn2),
            w1_1=w1_ffn1[l], w3_1=w3_ffn1[l], w2_1=w2_ffn1[l].astype(BF16),
            w1_2=w1_ffn2[l], w3_2=w3_ffn2[l], w2_2=w2_ffn2[l].astype(BF16),
            w_in=_prep_w_in(w_in[l], lay, N_IDX_HEADS), w_out=w_out[l].astype(BF16),
            gains=(row(qnorm_a), row(knorm_a), gki, row(qnorm_b), row(knorm_b)),
            lams=(row(lam_q1), row(lam_k1), row(lam_q2), row(lam_k2)),
            subln_g=row(subln_b),
        )
        mod = _ada(c_all, w_ada[l], b_ada[l][None, :])

        def mods_for(lo, nb, s):
            m = mod[lo:lo + nb].reshape(nb, N_MOD, D)
            if nb == 1:
                return tuple(m[:, i] for i in range(N_MOD))
            return tuple(jnp.repeat(m[:, i], s, axis=0) for i in range(N_MOD))

        xp, rp = _layer(xp, Bp, Sp, mods_for(0, Bp, Sp), None, 0, lw, l)
        past = (cache_k_a[l], cache_v_a[l], cache_k_idx[l], cache_k_b[l], cache_v_b[l])
        xs, rs = _layer(xs, Bs, Ss, mods_for(Bp, Bs, Ss), past, P, lw, l)
        rows_p.append(rp)
        rows_s.append(rs)

    def stack(rows, idx, B, S, tail):
        return jnp.stack([r[idx].reshape((B, S) + tail) for r in rows])

    tails = ((kv, hd), (kv, hd), (di,), (hb, 2, hd), (hb, db))
    outs_p = tuple(stack(rows_p, i, Bp, Sp, tails[i]) for i in range(5))
    outs_s = tuple(stack(rows_s, i, Bs, Ss, tails[i]) for i in range(5))
    return (xp.reshape(Bp, Sp, D), xs.reshape(Bs, Ss, D)) + outs_p + outs_s
```

```python
import functools
import math

import numpy as np
import jax
import jax.numpy as jnp
from jax import lax
from jax.experimental import pallas as pl
from jax.experimental.pallas import tpu as pltpu

F32 = jnp.float32
BF16 = jnp.bfloat16

CHUNK = 64
N_IDX_HEADS = 16
IDX_STACK = 4
TOPK_MAX = 256
ROPE_THETA = 500000.0
ROT_FRAC = 4
EPS = 1e-6
N_MOD = 9

LANES = 128
V7X_VMEM_BYTES = 64 * 1024 * 1024
VMEM_LIMIT = V7X_VMEM_BYTES * 7 // 8
NEG = -1e30
NT_DIMS = (((1,), (1,)), ((), ()))


def _lambda_init(layer_idx):
    return 0.8 - 0.6 * math.exp(-0.3 * layer_idx)


def _pick(n, candidates):
    for c in candidates:
        if n % c == 0:
            return c
    return n


def _cparams(semantics):
    return pltpu.CompilerParams(dimension_semantics=semantics, vmem_limit_bytes=VMEM_LIMIT)


def _round_up(n, m):
    return (n + m - 1) // m * m


def _ada_kernel(c_ref, w_ref, b_ref, o_ref):
    c = c_ref[...]
    s = (c * jax.nn.sigmoid(c)).astype(BF16)
    o_ref[...] = jnp.dot(s, w_ref[...].astype(BF16), preferred_element_type=F32) + b_ref[...]


def _ada(c, w, b):
    R, D = c.shape
    N = w.shape[1]
    tn = _pick(N, (512, 256, 128))
    return pl.pallas_call(
        _ada_kernel,
        grid=(N // tn,),
        in_specs=[pl.BlockSpec((R, D), lambda j: (0, 0)),
                  pl.BlockSpec((D, tn), lambda j: (0, j)),
                  pl.BlockSpec((1, tn), lambda j: (0, j))],
        out_specs=pl.BlockSpec((R, tn), lambda j: (0, j)),
        out_shape=jax.ShapeDtypeStruct((R, N), F32),
        compiler_params=_cparams(("parallel",)),
        name="ada_matvec",
    )(c, w, b)


def _norm_mod_kernel(x_ref, g_ref, sc_ref, sh_ref, o_ref):
    x = x_ref[...]
    ms = jnp.mean(x * x, axis=-1, keepdims=True)
    y = x * lax.rsqrt(ms + EPS) * g_ref[...]
    o_ref[...] = (y * (1.0 + sc_ref[...]) + sh_ref[...]).astype(o_ref.dtype)


def _norm_mod(x, g, scale, shift):
    M, D = x.shape
    tm = _pick(M, (256, 128, 64, 32, 16, 8))

    def mspec(a):
        if a.shape[0] == 1:
            return pl.BlockSpec((1, D), lambda i: (0, 0))
        return pl.BlockSpec((tm, D), lambda i: (i, 0))

    return pl.pallas_call(
        _norm_mod_kernel,
        grid=(M // tm,),
        in_specs=[pl.BlockSpec((tm, D), lambda i: (i, 0)),
                  pl.BlockSpec((1, D), lambda i: (0, 0)),
                  mspec(scale), mspec(shift)],
        out_specs=pl.BlockSpec((tm, D), lambda i: (i, 0)),
        out_shape=jax.ShapeDtypeStruct((M, D), BF16),
        compiler_params=_cparams(("parallel",)),
        name="norm_mod",
    )(x, g, scale, shift)


def _mm_up_kernel(h_ref, w1_ref, w3_ref, o_ref):
    h = h_ref[...]
    u = jnp.dot(h, w1_ref[...].astype(BF16), preferred_element_type=F32)
    v = jnp.dot(h, w3_ref[...].astype(BF16), preferred_element_type=F32)
    o_ref[...] = (u * jax.nn.sigmoid(u) * v).astype(o_ref.dtype)


def _mm_up(h, w1, w3):
    M, K = h.shape
    F = w1.shape[1]
    tm = _pick(M, (2048, 1024, 512, 256, 128, 64, 32, 16, 8))
    tn = _pick(F, (256, 128))
    return pl.pallas_call(
        _mm_up_kernel,
        grid=(M // tm, F // tn),
        in_specs=[pl.BlockSpec((tm, K), lambda i, j: (i, 0), pipeline_mode=pl.Buffered(1)),
                  pl.BlockSpec((K, tn), lambda i, j: (0, j)),
                  pl.BlockSpec((K, tn), lambda i, j: (0, j))],
        out_specs=pl.BlockSpec((tm, tn), lambda i, j: (i, j)),
        out_shape=jax.ShapeDtypeStruct((M, F), BF16),
        compiler_params=_cparams(("parallel", "parallel")),
        name="ffn_up",
    )(h, w1, w3)


def _mm_plain_kernel(h_ref, w_ref, o_ref):
    o_ref[...] = jnp.dot(h_ref[...], w_ref[...], preferred_element_type=F32)


def _mm_plain(h, w):
    M, K = h.shape
    N = w.shape[1]
    tm = _pick(M, (1024, 512, 256, 128, 64, 32, 16, 8))
    tn = _pick(N, (512, 256, 128))
    return pl.pallas_call(
        _mm_plain_kernel,
        grid=(M // tm, N // tn),
        in_specs=[pl.BlockSpec((tm, K), lambda i, j: (i, 0)),
                  pl.BlockSpec((K, tn), lambda i, j: (0, j))],
        out_specs=pl.BlockSpec((tm, tn), lambda i, j: (i, j)),
        out_shape=jax.ShapeDtypeStruct((M, N), F32),
        compiler_params=_cparams(("parallel", "parallel")),
        name="in_proj",
    )(h, w)


def _mm_res_kernel(*refs, n_lhs, coef):
    a_refs = refs[:n_lhs]
    w_refs = refs[n_lhs:2 * n_lhs]
    x_ref, g_ref, o_ref = refs[2 * n_lhs:]
    y = jnp.dot(a_refs[0][...], w_refs[0][...], preferred_element_type=F32)
    for a_ref, w_ref in zip(a_refs[1:], w_refs[1:]):
        y = y + jnp.dot(a_ref[...], w_ref[...], preferred_element_type=F32)
    o_ref[...] = x_ref[...] + (coef * g_ref[...]) * y


def _mm_res(a_list, w_list, x, gate, coef, name):
    M, N = x.shape
    ktot = sum(a.shape[1] for a in a_list)
    tm = _pick(M, (1024, 512, 256, 128, 64, 32, 16, 8))
    tn = _pick(N, (512, 256, 128)) if ktot <= 4096 else _pick(N, (256, 128))
    lhs_mode = {} if ktot <= 4096 else dict(pipeline_mode=pl.Buffered(1))
    in_specs = [pl.BlockSpec((tm, a.shape[1]), lambda i, j: (i, 0), **lhs_mode) for a in a_list]
    in_specs += [pl.BlockSpec((w.shape[0], tn), lambda i, j: (0, j)) for w in w_list]
    in_specs.append(pl.BlockSpec((tm, tn), lambda i, j: (i, j)))
    if gate.shape[0] == 1:
        in_specs.append(pl.BlockSpec((1, tn), lambda i, j: (0, j)))
    else:
        in_specs.append(pl.BlockSpec((tm, tn), lambda i, j: (i, j)))
    return pl.pallas_call(
        functools.partial(_mm_res_kernel, n_lhs=len(a_list), coef=coef),
        grid=(M // tm, N // tn),
        in_specs=in_specs,
        out_specs=pl.BlockSpec((tm, tn), lambda i, j: (i, j)),
        out_shape=jax.ShapeDtypeStruct((M, N), F32),
        compiler_params=_cparams(("parallel", "parallel")),
        name=name,
    )(*a_list, *w_list, x, gate)


class _Layout:
    def __init__(self, ha, kv, hb, hd, di):
        self.ha, self.kv, self.hb, self.hd, self.di = ha, kv, hb, hd, di
        self.q_a = 0
        self.k_a = self.q_a + ha * hd
        self.v_a = self.k_a + kv * hd
        self.q_i = self.v_a + kv * hd
        self.kiw = self.q_i + N_IDX_HEADS * di
        self.q_b = self.kiw + LANES
        self.k_b = self.q_b + hb * 2 * hd
        self.v_b = self.k_b + hb * 2 * hd
        self.end = self.v_b + hb * 2 * hd
        self.width = _round_up(self.end, 512)


def _rope_tables(hd, di):
    def table(d):
        r = d // ROT_FRAC
        half = r // 2
        inv = ROPE_THETA ** (-(jnp.arange(half, dtype=jnp.float32) * 2.0 / r))
        lane = np.arange(LANES) % d
        idx = np.where(lane < half, lane, np.where(lane < r, lane - half, 0))
        invl = jnp.where(lane < r, inv[idx], 0.0)
        sign = np.where(lane < half, -1.0, np.where(lane < r, 1.0, 0.0)).astype(np.float32)
        src = np.where(lane < half, np.arange(LANES) + half, np.arange(LANES) - half)
        partner = np.zeros((LANES, LANES), np.float32)
        partner[src[lane < r], np.arange(LANES)[lane < r]] = 1.0
        return jnp.stack([invl, jnp.asarray(sign)])[:, None, :], jnp.asarray(partner, dtype=BF16)
    th, ph = table(hd)
    ti, pi = table(di)
    return th, ph, ti, pi


def _split3(x):
    x1 = x.astype(BF16)
    r1 = x - x1.astype(F32)
    x2 = r1.astype(BF16)
    x3 = (r1 - x2.astype(F32)).astype(BF16)
    return x1, x2, x3


def _post_kernel(z_ref, gqa_ref, gka_ref, gki_ref, gqb_ref, gkb_ref, th_ref, ph_ref, ti_ref, pi_ref,
                 ka_ref, va_ref, ki_ref, kb_ref, vb_ref,
                 qa16_ref, ka16_ref, va16_ref, qi16_ref, kiw_ref, kiw16_ref, qb16_ref, kb16_ref, vb16_ref,
                 *, lay, ts, pos_off, scale):
    hd, di = lay.hd, lay.di
    row = lax.broadcasted_iota(jnp.int32, (ts, LANES), 0)
    pos = (pos_off + pl.program_id(1) * ts + row).astype(F32)

    def rope_fn(t_ref, p_ref):
        ang = pos * t_ref[0]
        cosv = jnp.cos(ang)
        sinv = jnp.sin(ang) * t_ref[1]

        def rope(x):
            partner = sum(jnp.dot(t, p_ref[...], preferred_element_type=F32) for t in _split3(x))
            return x * cosv + partner * sinv
        return rope

    rope_h = rope_fn(th_ref, ph_ref)
    rope_i = rope_fn(ti_ref, pi_ref)

    def hnorm(x, g):
        ms = jnp.mean(x * x, axis=-1, keepdims=True)
        return x * lax.rsqrt(ms + EPS) * g

    def slab(off, h):
        return z_ref[0, :, off + h * hd: off + (h + 1) * hd]

    def cols(h):
        return slice(h * hd, (h + 1) * hd)

    for h in range(lay.ha):
        q = rope_h(hnorm(slab(lay.q_a, h), gqa_ref[...]))
        qa16_ref[0, :, cols(h)] = (q * scale).astype(BF16)
    for h in range(lay.kv):
        k = rope_h(hnorm(slab(lay.k_a, h), gka_ref[...]))
        ka_ref[0, :, cols(h)] = k
        ka16_ref[0, :, cols(h)] = k.astype(BF16)
        v = slab(lay.v_a, h)
        va_ref[0, :, cols(h)] = v
        va16_ref[0, :, cols(h)] = v.astype(BF16)
    for h in range(N_IDX_HEADS * di // LANES):
        qi16_ref[0, :, cols(h)] = rope_i(slab(lay.q_i, h)).astype(BF16)
    x = slab(lay.kiw, 0)
    lane = lax.broadcasted_iota(jnp.int32, (ts, LANES), 1)
    is_k = lane < di
    ms = jnp.sum(jnp.where(is_k, x * x, 0.0), axis=-1, keepdims=True) * (1.0 / di)
    ki = rope_i(x * lax.rsqrt(ms + EPS) * gki_ref[...])
    ki_ref[0] = ki[:, :di]
    kiw = jnp.where(is_k, ki, x)
    kiw_ref[0] = kiw
    kiw16_ref[0] = kiw.astype(BF16)
    for h in range(2 * lay.hb):
        q = rope_h(hnorm(slab(lay.q_b, h), gqb_ref[...]))
        qb16_ref[0, :, cols(h)] = (q * scale).astype(BF16)
        k = rope_h(hnorm(slab(lay.k_b, h), gkb_ref[...]))
        kb_ref[0, :, cols(h)] = k
        kb16_ref[0, :, cols(h)] = k.astype(BF16)
        v = slab(lay.v_b, h)
        vb_ref[0, :, cols(h)] = v
        vb16_ref[0, :, cols(h)] = v.astype(BF16)


def _post(z, lay, gains, tables, pos_off):
    B, S, W = z.shape
    hd, di = lay.hd, lay.di
    ts = _pick(S, (128, 64, 32, 16, 8))
    wa, wkv, wqi, wb = lay.ha * hd, lay.kv * hd, N_IDX_HEADS * di, lay.hb * 2 * hd

    def rows(width):
        return pl.BlockSpec((1, ts, width), lambda b, s: (b, s, 0))

    def const(a):
        return pl.BlockSpec(a.shape, lambda b, s: (0,) * a.ndim)

    out_widths = [(wkv, F32), (wkv, F32), (di, F32), (wb, F32), (wb, F32),
                  (wa, BF16), (wkv, BF16), (wkv, BF16), (wqi, BF16), (LANES, F32), (LANES, BF16),
                  (wb, BF16), (wb, BF16), (wb, BF16)]
    return pl.pallas_call(
        functools.partial(_post_kernel, lay=lay, ts=ts, pos_off=pos_off,
                          scale=hd ** -0.5 * math.log2(math.e)),
        grid=(B, S // ts),
        in_specs=[rows(W)] + [const(g) for g in gains] + [const(t) for t in tables],
        out_specs=[rows(w) for w, _ in out_widths],
        out_shape=[jax.ShapeDtypeStruct((B, S, w), dt) for w, dt in out_widths],
        compiler_params=_cparams(("parallel", "parallel")),
        name="head_norm_rope",
    )(z, *gains, *tables)


def _tile_bounds(p0, tq, tk, n_keys):
    a_full = jnp.minimum((p0 // CHUNK + 1) * CHUNK, n_keys)
    a_tot = jnp.minimum(((p0 + tq - 1) // CHUNK + 1) * CHUNK, n_keys)
    return a_full // tk, (a_tot + tk - 1) // tk


def _allowed(j, p0, tq, tk, n_keys):
    shift = CHUNK.bit_length() - 1
    kpos = j * tk + lax.broadcasted_iota(jnp.int32, (tq, tk), 1)
    qpos = p0 + lax.broadcasted_iota(jnp.int32, (tq, tk), 0)
    return (lax.shift_right_logical(kpos, shift) <= lax.shift_right_logical(qpos, shift)) & (kpos < n_keys)


def _lane_tile_list(parts):
    return parts[0] if len(parts) == 1 else jnp.concatenate(parts, axis=1)


def _lane_tile(x, reps):
    return _lane_tile_list([x] * reps)


def _key_tile(lp):
    return 1024 if lp % 1024 == 0 else 256


def _lane_tile_max(s):
    out = s[:, :LANES]
    for c in range(1, s.shape[1] // LANES):
        out = jnp.maximum(out, s[:, c * LANES:(c + 1) * LANES])
    return out


def _softmax_step(s, v, m_ref, l_ref, acc_ref, idx, tile_max=None):
    tk = s.shape[1]
    wv = acc_ref.shape[-1]
    m_old = m_ref[idx]
    if tile_max is None:
        tile_max = _lane_tile_max(s)
    m_new = jnp.maximum(m_old, jnp.max(tile_max, axis=1, keepdims=True))
    alpha = jnp.exp2(m_old - m_new)
    p = jnp.exp2(s - _lane_tile(m_new, tk // LANES))
    if l_ref is not None:
        l_ref[idx] = alpha * l_ref[idx] + jnp.sum(p, axis=1, keepdims=True)
    acc_ref[idx] = _lane_tile(alpha, wv // LANES) * acc_ref[idx] + jnp.dot(
        p.astype(BF16), v, preferred_element_type=F32)
    m_ref[idx] = m_new


def _dsa_kernel(qa_ref, qi_ref, wq_ref, ka_ref, va_ref, ki_ref, o_ref,
                keys_ref, wexp_ref, qih_ref, tau_ref, bias_ref, m_ref, acc_ref, s_ref, mx_ref,
                *, tq, tk, n_keys, pos_off, topk, ha, kv, hd, di):
    p0 = pos_off + pl.program_id(1) * tq
    n_full, n_tot = _tile_bounds(p0, tq, tk, n_keys)
    reps = tk // LANES
    group = ha // kv
    int_min = jnp.int32(-2 ** 31)

    def rows(r):
        return slice(r * tq, (r + 1) * tq)

    lane = lax.broadcasted_iota(jnp.int32, (tq, LANES), 1)
    w_scale = (N_IDX_HEADS ** -0.5) * (di ** -0.5)
    for h in range(N_IDX_HEADS):
        pair = qi_ref[0, :, (h * di // LANES) * LANES:(h * di // LANES + 1) * LANES].astype(F32)
        off = (h * di) % LANES
        if off:
            pair = pltpu.roll(pair, LANES - off, 1)
        qih_ref[h // IDX_STACK, rows(h % IDX_STACK), :] = jnp.where(lane < di, pair, 0.0).astype(BF16)
        wexp_ref[h] = jnp.broadcast_to(wq_ref[0, :, di + h:di + h + 1] * w_scale, (tq, LANES))

    def key_slice(j):
        return pl.ds(pl.multiple_of(j * tk, tk), tk)

    def score_tile(j, masked):
        kt = ki_ref[0, key_slice(j), :]
        parts = [jnp.zeros((tq, LANES), F32)] * reps
        for hs in range(N_IDX_HEADS // IDX_STACK):
            lg = lax.dot_general(qih_ref[hs], kt, NT_DIMS, preferred_element_type=F32)
            for r in range(IDX_STACK):
                w = wexp_ref[hs * IDX_STACK + r]
                parts = [parts[c] + jnp.maximum(lg[rows(r), c * LANES:(c + 1) * LANES], 0.0) * w
                         for c in range(reps)]
        sc = _lane_tile_list(parts)
        if masked:
            sc = jnp.where(_allowed(j, p0, tq, tk, n_keys), sc, -jnp.inf)
        bits = pltpu.bitcast(sc, jnp.int32)
        keys_ref[j] = jnp.where(bits < 0, bits ^ jnp.int32(0x7FFFFFFF), bits)

    def score_full(j, c):
        score_tile(j, False)
        return c

    def score_masked(j, c):
        score_tile(j, True)
        return c

    lax.fori_loop(0, n_full, score_full, 0)
    lax.fori_loop(n_full, n_tot, score_masked, 0)

    def count_ge(cand):
        def body(j, acc):
            t = keys_ref[j]
            for c in range(reps):
                acc = jnp.where(t[:, c * LANES:(c + 1) * LANES] >= cand, acc + 1.0, acc)
            return acc
        acc = lax.fori_loop(0, n_tot, body, jnp.zeros((tq, LANES), F32))
        return jnp.sum(acc, axis=1, keepdims=True)

    kf = float(topk)
    zero = jnp.zeros((tq, LANES), jnp.int32)
    c0 = jnp.broadcast_to(count_ge(zero), (tq, LANES))
    n_all = jnp.broadcast_to((n_tot * tk).astype(F32), (tq, LANES))
    state = (jnp.int32(0), jnp.where(c0 >= kf, zero, int_min), jnp.where(c0 >= kf, c0, n_all))

    def bit_cond(state):
        b, _, cnt = state
        return (b < 31) & (jnp.max(jnp.where(cnt == kf, 0.0, 1.0)) > 0.0)

    def bit_body(state):
        b, prefix, cnt = state
        cand = prefix + lax.shift_left(jnp.int32(1), 30 - b)
        c = jnp.broadcast_to(count_ge(cand), (tq, LANES))
        return b + 1, jnp.where(c >= kf, cand, prefix), jnp.where(c >= kf, c, cnt)

    tau_ref[...] = lax.while_loop(bit_cond, bit_body, state)[1]

    m_ref[...] = jnp.full(m_ref.shape, NEG, F32)
    acc_ref[...] = jnp.zeros(acc_ref.shape, F32)
    ones = jnp.ones((tk, hd), BF16)

    def attn_tile(j, masked):
        sel = keys_ref[j] >= _lane_tile(tau_ref[...], reps)
        if masked:
            sel = sel & _allowed(j, p0, tq, tk, n_keys)
        bias_ref[...] = jnp.where(sel, 0.0, NEG)
        def produce(h):
            k = ka_ref[0, key_slice(j), (h // group) * hd:(h // group + 1) * hd]
            s = lax.dot_general(qa_ref[0, :, h * hd:(h + 1) * hd], k, NT_DIMS, preferred_element_type=F32)
            s = s + bias_ref[...]
            s_ref[h % 2] = s
            mx_ref[h % 2] = _lane_tile_max(s)

        def consume(h):
            g = h // group
            v1 = jnp.concatenate([va_ref[0, key_slice(j), g * hd:(g + 1) * hd], ones], axis=1)
            _softmax_step(s_ref[h % 2], v1, m_ref, None, acc_ref, h, tile_max=mx_ref[h % 2])

        produce(0)
        for h in range(ha):
            if h + 1 < ha:
                produce(h + 1)
            consume(h)

    def attn_full(j, c):
        attn_tile(j, False)
        return c

    def attn_masked(j, c):
        attn_tile(j, True)
        return c

    lax.fori_loop(0, n_full, attn_full, 0)
    lax.fori_loop(n_full, n_tot, attn_masked, 0)
    for h in range(ha):
        o_ref[0, :, h * hd:(h + 1) * hd] = (acc_ref[h, :, :hd] / acc_ref[h, :, hd:]).astype(o_ref.dtype)


def _dsa(qa, qi, kiw_q, ka, va, kiw_k, n_keys, pos_off, kv, hd, di):
    B, S, wa = qa.shape
    lp = ka.shape[1]
    ha = wa // hd
    tq = _pick(S, (128, 64, 32, 16, 8))
    tk = _key_tile(lp)
    topk = min(TOPK_MAX, n_keys // 4)
    kern = functools.partial(_dsa_kernel, tq=tq, tk=tk, n_keys=n_keys, pos_off=pos_off, topk=topk,
                             ha=ha, kv=kv, hd=hd, di=di)

    def rows(width):
        return pl.BlockSpec((1, tq, width), lambda b, i: (b, i, 0))

    def resident(width):
        return pl.BlockSpec((1, lp, width), lambda b, i: (b, 0, 0), pipeline_mode=pl.Buffered(1))

    return pl.pallas_call(
        kern,
        grid=(B, S // tq),
        in_specs=[rows(wa), rows(qi.shape[2]), rows(LANES),
                  resident(kv * hd), resident(kv * hd), resident(LANES)],
        out_specs=rows(wa),
        out_shape=jax.ShapeDtypeStruct((B, S, wa), BF16),
        scratch_shapes=[pltpu.VMEM((lp // tk, tq, tk), jnp.int32),
                        pltpu.VMEM((N_IDX_HEADS, tq, LANES), F32),
                        pltpu.VMEM((N_IDX_HEADS // IDX_STACK, IDX_STACK * tq, LANES), BF16),
                        pltpu.VMEM((tq, LANES), jnp.int32),
                        pltpu.VMEM((tq, tk), F32),
                        pltpu.VMEM((ha, tq, LANES), F32),
                        pltpu.VMEM((ha, tq, 2 * hd), F32),
                        pltpu.VMEM((2, tq, tk), F32),
                        pltpu.VMEM((2, tq, LANES), F32)],
        compiler_params=_cparams(("parallel", "arbitrary")),
        name="dsa_attention",
    )(qa, qi, kiw_q, ka, va, kiw_k)


def _diff_kernel(q_ref, k_ref, v_ref, lq1_ref, lk1_ref, lq2_ref, lk2_ref, sg_ref, o_ref,
                 m_ref, l_ref, acc_ref, s_ref, mx_ref, *, tq, tk, n_keys, pos_off, hd, lam_init):
    p0 = pos_off + pl.program_id(2) * tq
    n_full, n_tot = _tile_bounds(p0, tq, tk, n_keys)
    db = 2 * hd
    m_ref[...] = jnp.full(m_ref.shape, NEG, F32)
    l_ref[...] = jnp.zeros(l_ref.shape, F32)
    acc_ref[...] = jnp.zeros(acc_ref.shape, F32)

    def key_slice(j):
        return pl.ds(pl.multiple_of(j * tk, tk), tk)

    def scores(j, c):
        return lax.dot_general(q_ref[0, :, c * hd:(c + 1) * hd], k_ref[0, key_slice(j), c * hd:(c + 1) * hd],
                               NT_DIMS, preferred_element_type=F32)

    def produce(j, slot, masked=False):
        ok = _allowed(j, p0, tq, tk, n_keys) if masked else None
        for c in range(2):
            s = scores(j, c)
            if masked:
                s = jnp.where(ok, s, NEG)
            s_ref[slot, c] = s
            mx_ref[slot, c] = _lane_tile_max(s)

    def consume(j, slot):
        v = v_ref[0, key_slice(j), :]
        for c in range(2):
            _softmax_step(s_ref[slot, c], v, m_ref, l_ref, acc_ref, c, tile_max=mx_ref[slot, c])

    n_pipe =jnp.minimum(n_full + 1, n_tot)
    n_pairs = jnp.maximum(n_pipe - 2, 0) // 2
    first = 2 * n_pairs
    left = n_pipe - first

    @pl.when(n_pipe >= 2)
    def _():
        produce(0, 0)

    def pair_body(t, carry):
        produce(2 * t + 1, 1)
        consume(2 * t, 0)
        produce(2 * t + 2, 0)
        consume(2 * t + 1, 1)
        return carry

    lax.fori_loop(0, n_pairs, pair_body, 0)

    @pl.when(left == 1)
    def _():
        produce(first, 0, masked=True)
        consume(first, 0)

    @pl.when(left == 2)
    def _():
        produce(first + 1, 1, masked=True)
        consume(first, 0)
        consume(first + 1, 1)

    @pl.when(left == 3)
    def _():
        produce(first + 1, 1)
        consume(first, 0)
        produce(first + 2, 0, masked=True)
        consume(first + 1, 1)
        consume(first + 2, 0)

    def tile_masked(j, carry):
        produce(j, 0, masked=True)
        consume(j, 0)
        return carry

    lax.fori_loop(n_pipe, n_tot, tile_masked, 0)

    lam = (jnp.exp(jnp.sum(lq1_ref[...] * lk1_ref[...], axis=-1, keepdims=True))
           - jnp.exp(jnp.sum(lq2_ref[...] * lk2_ref[...], axis=-1, keepdims=True)) + lam_init)
    reps = db // LANES
    o = acc_ref[0] / _lane_tile(l_ref[0], reps) - lam * (acc_ref[1] / _lane_tile(l_ref[1], reps))
    ms = jnp.mean(o * o, axis=-1, keepdims=True)
    o_ref[0] = (o * lax.rsqrt(ms + EPS) * sg_ref[...] * (1.0 - lam_init)).astype(o_ref.dtype)


def _diff(qb, kb, vb, lams, subln_g, n_keys, pos_off, hd, lam_init):
    B, S, wb = qb.shape
    lp = kb.shape[1]
    db = 2 * hd
    hb = wb // db
    tq = _pick(S, (512, 256, 128, 64, 32, 16, 8))
    tk = _key_tile(lp)
    kern = functools.partial(_diff_kernel, tq=tq, tk=tk, n_keys=n_keys, pos_off=pos_off, hd=hd,
                             lam_init=lam_init)
    vec = pl.BlockSpec((1, hd), lambda b, h, i: (0, 0))
    return pl.pallas_call(
        kern,
        grid=(B, hb, S // tq),
        in_specs=[pl.BlockSpec((1, tq, db), lambda b, h, i: (b, i, h)),
                  pl.BlockSpec((1, lp, db), lambda b, h, i: (b, 0, h)),
                  pl.BlockSpec((1, lp, db), lambda b, h, i: (b, 0, h)),
                  vec, vec, vec, vec,
                  pl.BlockSpec((1, db), lambda b, h, i: (0, 0))],
        out_specs=pl.BlockSpec((1, tq, db), lambda b, h, i: (b, i, h)),
        out_shape=jax.ShapeDtypeStruct((B, S, wb), BF16),
        scratch_shapes=[pltpu.VMEM((2, tq, LANES), F32),
                        pltpu.VMEM((2, tq, LANES), F32),
                        pltpu.VMEM((2, tq, db), F32),
                        pltpu.VMEM((2, 2, tq, tk), F32),
                        pltpu.VMEM((2, 2, tq, LANES), F32)],
        compiler_params=_cparams(("parallel", "parallel", "arbitrary")),
        name="diff_attention",
    )(qb, kb, vb, *lams, subln_g)


def _pad_rows(a, lp):
    return jnp.pad(a, ((0, 0), (0, lp - a.shape[1]), (0, 0)))


def _prep_w_in(w_in, lay, d_idx_w):
    ha, kv, hb, hd, di = lay.ha, lay.kv, lay.hb, lay.hd, lay.di
    widths = (ha * hd, kv * hd, kv * hd, N_IDX_HEADS * di, di, d_idx_w, hb * 2 * hd, hb * 2 * hd, hb * 2 * hd)
    offs = np.cumsum((0,) + widths)
    seg = [w_in[:, offs[i]:offs[i + 1]] for i in range(len(widths))]
    d = w_in.shape[0]
    pad_kiw = jnp.zeros((d, LANES - di - d_idx_w), w_in.dtype)
    pad_end = jnp.zeros((d, lay.width - lay.end), w_in.dtype)
    return jnp.concatenate(seg[:6] + [pad_kiw] + seg[6:] + [pad_end], axis=1).astype(BF16)


def _layer(x, B, S, mods, past, pos_off, lw, li):
    (sh1, sc1, gt1, sh2, sc2, gt2, sh3, sc3, gt3) = mods
    D = x.shape[1]
    lay = lw["lay"]
    hd, di, kv = lay.hd, lay.di, lay.kv

    h = _norm_mod(x, lw["norm_ffn1"], sc1, sh1)
    a = _mm_up(h, lw["w1_1"], lw["w3_1"])
    x = _mm_res([a], [lw["w2_1"]], x, gt1, 0.5, "ffn_down")

    h = _norm_mod(x, lw["norm_attn"], sc2, sh2)
    z = _mm_plain(h, lw["w_in"]).reshape(B, S, lay.width)
    (k_a, v_a, k_i, k_b, v_b, qa16, ka16, va16, qi16, kiw, kiw16, qb16, kb16, vb16) = _post(
        z, lay, lw["gains"], lw["tables"], pos_off)

    if past is not None:
        ck_a, cv_a, ck_i, ck_b, cv_b = past
        P = ck_a.shape[1]
        flat = lambda c: c.reshape(B, P, -1).astype(BF16)
        ka16 = jnp.concatenate([flat(ck_a), ka16], axis=1)
        va16 = jnp.concatenate([flat(cv_a), va16], axis=1)
        ck_i16 = jnp.pad(flat(ck_i), ((0, 0), (0, 0), (0, LANES - di)))
        kiw16_k = jnp.concatenate([ck_i16, kiw16], axis=1)
        kb16 = jnp.concatenate([flat(ck_b), kb16], axis=1)
        vb16 = jnp.concatenate([flat(cv_b), vb16], axis=1)
        n_keys = P + S
    else:
        kiw16_k = kiw16
        n_keys = S
    lp = _round_up(n_keys, 256)
    if lp != n_keys:
        ka16, va16, kiw16_k, kb16, vb16 = (_pad_rows(t, lp) for t in (ka16, va16, kiw16_k, kb16, vb16))

    oa = _dsa(qa16, qi16, kiw, ka16, va16, kiw16_k, n_keys, pos_off, kv, hd, di)
    ob = _diff(qb16, kb16, vb16, lw["lams"], lw["subln_g"], n_keys, pos_off, hd, _lambda_init(li))
    wa = oa.shape[2]
    x = _mm_res([oa.reshape(B * S, wa), ob.reshape(B * S, -1)], [lw["w_out"][:wa], lw["w_out"][wa:]],
                x, gt2, 1.0, "out_proj")

    h = _norm_mod(x, lw["norm_ffn2"], sc3, sh3)
    a = _mm_up(h, lw["w1_2"], lw["w3_2"])
    x = _mm_res([a], [lw["w2_2"]], x, gt3, 0.5, "ffn_down")
    return x, (k_a, v_a, k_i, k_b, v_b)


def kernel(x_prompt, x_sample, cache_k_a, cache_v_a, cache_k_idx, cache_k_b, cache_v_b, c_prompt, c_sample, w_ada, b_ada, norm_ffn1, w1_ffn1, w3_ffn1, w2_ffn1, norm_attn, w_in, qnorm_a, knorm_a, knorm_idx, qnorm_b, knorm_b, lam_q1, lam_k1, lam_q2, lam_k2, subln_b, w_out, norm_ffn2, w1_ffn2, w3_ffn2, w2_ffn2):
    depth = w_ada.shape[0]
    Bp, Sp, D = x_prompt.shape
    Bs, Ss, _ = x_sample.shape
    P = cache_k_a.shape[2]
    kv, hd = cache_k_a.shape[3], cache_k_a.shape[4]
    di = cache_k_idx.shape[3]
    hb = cache_k_b.shape[3]
    db = cache_v_b.shape[4]
    ha = (w_out.shape[1] - hb * db) // hd
    lay = _Layout(ha, kv, hb, hd, di)
    tables = _rope_tables(hd, di)

    xp = x_prompt.reshape(Bp * Sp, D)
    xs = x_sample.reshape(Bs * Ss, D)
    n_c = Bp + Bs
    c_all = jnp.pad(jnp.concatenate([c_prompt, c_sample], axis=0), ((0, _round_up(n_c, 8) - n_c), (0, 0)))

    rows_p, rows_s = [], []
    for l in range(depth):
        row = lambda a: a[l][None, :]
        gki = jnp.pad(knorm_idx[l], (0, LANES - di))[None, :]
        lw = dict(
            lay=lay, tables=tables,
            norm_ffn1=row(norm_ffn1), norm_attn=row(norm_attn), norm_ffn2=row(norm_ffn2),
            w1_1=w1_ffn1[l], w3_1=w3_ffn1[l], w2_1=w2_ffn1[l].astype(BF16),
            w1_2=w1_ffn2[l], w3_2=w3_ffn2[l], w2_2=w2_ffn2[l].astype(BF16),
            w_in=_prep_w_in(w_in[l], lay, N_IDX_HEADS), w_out=w_out[l].astype(BF16),
            gains=(row(qnorm_a), row(knorm_a), gki, row(qnorm_b), row(knorm_b)),
            lams=(row(lam_q1), row(lam_k1), row(lam_q2), row(lam_k2)),
            subln_g=row(subln_b),
        )
        mod = _ada(c_all, w_ada[l], b_ada[l][None, :])

        def mods_for(lo, nb, s):
            m = mod[lo:lo + nb].reshape(nb, N_MOD, D)
            if nb == 1:
                return tuple(m[:, i] for i in range(N_MOD))
            return tuple(jnp.repeat(m[:, i], s, axis=0) for i in range(N_MOD))

        xp, rp = _layer(xp, Bp, Sp, mods_for(0, Bp, Sp), None, 0, lw, l)
        past = (cache_k_a[l], cache_v_a[l], cache_k_idx[l], cache_k_b[l], cache_v_b[l])
        xs, rs = _layer(xs, Bs, Ss, mods_for(Bp, Bs, Ss), past, P, lw, l)
        rows_p.append(rp)
        rows_s.append(rs)

    def stack(rows, idx, B, S, tail):
        return jnp.stack([r[idx].reshape((B, S) + tail) for r in rows])

    tails = ((kv, hd), (kv, hd), (di,), (hb, 2, hd), (hb, db))
    outs_p = tuple(stack(rows_p, i, Bp, Sp, tails[i]) for i in range(5))
    outs_s = tuple(stack(rows_s, i, Bs, Ss, tails[i]) for i in range(5))
    return (xp.reshape(Bp, Sp, D), xs.reshape(Bs, Ss, D)) + outs_p + outs_s
```

```python
import functools
import math

import numpy as np
import jax
import jax.numpy as jnp
from jax import lax
from jax.experimental import pallas as pl
from jax.experimental.pallas import tpu as pltpu

F32 = jnp.float32
BF16 = jnp.bfloat16

CHUNK = 64
N_IDX_HEADS = 16
IDX_STACK = 4
TOPK_MAX = 256
ROPE_THETA = 500000.0
ROT_FRAC = 4
EPS = 1e-6
N_MOD = 9

LANES = 128
V7X_VMEM_BYTES = 64 * 1024 * 1024
VMEM_LIMIT = V7X_VMEM_BYTES * 7 // 8
NEG = -1e30
NT_DIMS = (((1,), (1,)), ((), ()))


def _lambda_init(layer_idx):
    return 0.8 - 0.6 * math.exp(-0.3 * layer_idx)


def _pick(n, candidates):
    for c in candidates:
        if n % c == 0:
            return c
    return n


def _cparams(semantics):
    return pltpu.CompilerParams(dimension_semantics=semantics, vmem_limit_bytes=VMEM_LIMIT)


def _round_up(n, m):
    return (n + m - 1) // m * m


def _ada_kernel(c_ref, w_ref, b_ref, o_ref):
    c = c_ref[...]
    s = (c * jax.nn.sigmoid(c)).astype(BF16)
    o_ref[...] = jnp.dot(s, w_ref[...].astype(BF16), preferred_element_type=F32) + b_ref[...]


def _ada(c, w, b):
    R, D = c.shape
    N = w.shape[1]
    tn = _pick(N, (512, 256, 128))
    return pl.pallas_call(
        _ada_kernel,
        grid=(N // tn,),
        in_specs=[pl.BlockSpec((R, D), lambda j: (0, 0)),
                  pl.BlockSpec((D, tn), lambda j: (0, j)),
                  pl.BlockSpec((1, tn), lambda j: (0, j))],
        out_specs=pl.BlockSpec((R, tn), lambda j: (0, j)),
        out_shape=jax.ShapeDtypeStruct((R, N), F32),
        compiler_params=_cparams(("parallel",)),
        name="ada_matvec",
    )(c, w, b)


def _norm_mod_kernel(x_ref, g_ref, sc_ref, sh_ref, o_ref):
    x = x_ref[...]
    ms = jnp.mean(x * x, axis=-1, keepdims=True)
    y = x * lax.rsqrt(ms + EPS) * g_ref[...]
    o_ref[...] = (y * (1.0 + sc_ref[...]) + sh_ref[...]).astype(o_ref.dtype)


def _norm_mod(x, g, scale, shift):
    M, D = x.shape
    tm = _pick(M, (256, 128, 64, 32, 16, 8))

    def mspec(a):
        if a.shape[0] == 1:
            return pl.BlockSpec((1, D), lambda i: (0, 0))
        return pl.BlockSpec((tm, D), lambda i: (i, 0))

    return pl.pallas_call(
        _norm_mod_kernel,
        grid=(M // tm,),
        in_specs=[pl.BlockSpec((tm, D), lambda i: (i, 0)),
                  pl.BlockSpec((1, D), lambda i: (0, 0)),
                  mspec(scale), mspec(shift)],
        out_specs=pl.BlockSpec((tm, D), lambda i: (i, 0)),
        out_shape=jax.ShapeDtypeStruct((M, D), BF16),
        compiler_params=_cparams(("parallel",)),
        name="norm_mod",
    )(x, g, scale, shift)


def _mm_up_kernel(h_ref, w1_ref, w3_ref, o_ref):
    h = h_ref[...]
    u = jnp.dot(h, w1_ref[...].astype(BF16), preferred_element_type=F32)
    v = jnp.dot(h, w3_ref[...].astype(BF16), preferred_element_type=F32)
    o_ref[...] = (u * jax.nn.sigmoid(u) * v).astype(o_ref.dtype)


def _mm_up(h, w1, w3):
    M, K = h.shape
    F = w1.shape[1]
    tm = _pick(M, (2048, 1024, 512, 256, 128, 64, 32, 16, 8))
    tn = _pick(F, (256, 128))
    return pl.pallas_call(
        _mm_up_kernel,
        grid=(M // tm, F // tn),
        in_specs=[pl.BlockSpec((tm, K), lambda i, j: (i, 0), pipeline_mode=pl.Buffered(1)),
                  pl.BlockSpec((K, tn), lambda i, j: (0, j)),
                  pl.BlockSpec((K, tn), lambda i, j: (0, j))],
        out_specs=pl.BlockSpec((tm, tn), lambda i, j: (i, j)),
        out_shape=jax.ShapeDtypeStruct((M, F), BF16),
        compiler_params=_cparams(("parallel", "parallel")),
        name="ffn_up",
    )(h, w1, w3)


def _mm_plain_kernel(h_ref, w_ref, o_ref):
    o_ref[...] = jnp.dot(h_ref[...], w_ref[...], preferred_element_type=F32)


def _mm_plain(h, w):
    M, K = h.shape
    N = w.shape[1]
    tm = _pick(M, (1024, 512, 256, 128, 64, 32, 16, 8))
    tn = _pick(N, (512, 256, 128))
    return pl.pallas_call(
        _mm_plain_kernel,
        grid=(M // tm, N // tn),
        in_specs=[pl.BlockSpec((tm, K), lambda i, j: (i, 0)),
                  pl.BlockSpec((K, tn), lambda i, j: (0, j))],
        out_specs=pl.BlockSpec((tm, tn), lambda i, j: (i, j)),
        out_shape=jax.ShapeDtypeStruct((M, N), F32),
        compiler_params=_cparams(("parallel", "parallel")),
        name="in_proj",
    )(h, w)


def _mm_res_kernel(*refs, n_lhs, coef):
    a_refs = refs[:n_lhs]
    w_refs = refs[n_lhs:2 * n_lhs]
    x_ref, g_ref, o_ref = refs[2 * n_lhs:]
    y = jnp.dot(a_refs[0][...], w_refs[0][...], preferred_element_type=F32)
    for a_ref, w_ref in zip(a_refs[1:], w_refs[1:]):
        y = y + jnp.dot(a_ref[...], w_ref[...], preferred_element_type=F32)
    o_ref[...] = x_ref[...] + (coef * g_ref[...]) * y


def _mm_res(a_list, w_list, x, gate, coef, name):
    M, N = x.shape
    ktot = sum(a.shape[1] for a in a_list)
    tm = _pick(M, (1024, 512, 256, 128, 64, 32, 16, 8))
    tn = _pick(N, (512, 256, 128)) if ktot <= 4096 else _pick(N, (256, 128))
    lhs_mode = {} if ktot <= 4096 else dict(pipeline_mode=pl.Buffered(1))
    in_specs = [pl.BlockSpec((tm, a.shape[1]), lambda i, j: (i, 0), **lhs_mode) for a in a_list]
    in_specs += [pl.BlockSpec((w.shape[0], tn), lambda i, j: (0, j)) for w in w_list]
    in_specs.append(pl.BlockSpec((tm, tn), lambda i, j: (i, j)))
    if gate.shape[0] == 1:
        in_specs.append(pl.BlockSpec((1, tn), lambda i, j: (0, j)))
    else:
        in_specs.append(pl.BlockSpec((tm, tn), lambda i, j: (i, j)))
    return pl.pallas_call(
        functools.partial(_mm_res_kernel, n_lhs=len(a_list), coef=coef),
        grid=(M // tm, N // tn),
        in_specs=in_specs,
        out_specs=pl.BlockSpec((tm, tn), lambda i, j: (i, j)),
        out_shape=jax.ShapeDtypeStruct((M, N), F32),
        compiler_params=_cparams(("parallel", "parallel")),
        name=name,
    )(*a_list, *w_list, x, gate)


class _Layout:
    def __init__(self, ha, kv, hb, hd, di):
        self.ha, self.kv, self.hb, self.hd, self.di = ha, kv, hb, hd, di
        self.q_a = 0
        self.k_a = self.q_a + ha * hd
        self.v_a = self.k_a + kv * hd
        self.q_i = self.v_a + kv * hd
        self.kiw = self.q_i + N_IDX_HEADS * di
        self.q_b = self.kiw + LANES
        self.k_b = self.q_b + hb * 2 * hd
        self.v_b = self.k_b + hb * 2 * hd
        self.end = self.v_b + hb * 2 * hd
        self.width = _round_up(self.end, 512)


def _rope_tables(hd, di):
    def table(d):
        r = d // ROT_FRAC
        half = r // 2
        inv = ROPE_THETA ** (-(jnp.arange(half, dtype=jnp.float32) * 2.0 / r))
        lane = np.arange(LANES) % d
        idx = np.where(lane < half, lane, np.where(lane < r, lane - half, 0))
        invl = jnp.where(lane < r, inv[idx], 0.0)
        sign = np.where(lane < half, -1.0, np.where(lane < r, 1.0, 0.0)).astype(np.float32)
        src = np.where(lane < half, np.arange(LANES) + half, np.arange(LANES) - half)
        partner = np.zeros((LANES, LANES), np.float32)
        partner[src[lane < r], np.arange(LANES)[lane < r]] = 1.0
        return jnp.stack([invl, jnp.asarray(sign)])[:, None, :], jnp.asarray(partner, dtype=BF16)
    th, ph = table(hd)
    ti, pi = table(di)
    return th, ph, ti, pi


def _split3(x):
    x1 = x.astype(BF16)
    r1 = x - x1.astype(F32)
    x2 = r1.astype(BF16)
    x3 = (r1 - x2.astype(F32)).astype(BF16)
    return x1, x2, x3


def _post_kernel(z_ref, gqa_ref, gka_ref, gki_ref, gqb_ref, gkb_ref, th_ref, ph_ref, ti_ref, pi_ref,
                 ka_ref, va_ref, ki_ref, kb_ref, vb_ref,
                 qa16_ref, ka16_ref, va16_ref, qi16_ref, kiw_ref, kiw16_ref, qb16_ref, kb16_ref, vb16_ref,
                 *, lay, ts, pos_off, scale):
    hd, di = lay.hd, lay.di
    row = lax.broadcasted_iota(jnp.int32, (ts, LANES), 0)
    pos = (pos_off + pl.program_id(1) * ts + row).astype(F32)

    def rope_fn(t_ref, p_ref):
        ang = pos * t_ref[0]
        cosv = jnp.cos(ang)
        sinv = jnp.sin(ang) * t_ref[1]

        def rope(x):
            partner = sum(jnp.dot(t, p_ref[...], preferred_element_type=F32) for t in _split3(x))
            return x * cosv + partner * sinv
        return rope

    rope_h = rope_fn(th_ref, ph_ref)
    rope_i = rope_fn(ti_ref, pi_ref)

    def hnorm(x, g):
        ms = jnp.mean(x * x, axis=-1, keepdims=True)
        return x * lax.rsqrt(ms + EPS) * g

    def slab(off, h):
        return z_ref[0, :, off + h * hd: off + (h + 1) * hd]

    def cols(h):
        return slice(h * hd, (h + 1) * hd)

    for h in range(lay.ha):
        q = rope_h(hnorm(slab(lay.q_a, h), gqa_ref[...]))
        qa16_ref[0, :, cols(h)] = (q * scale).astype(BF16)
    for h in range(lay.kv):
        k = rope_h(hnorm(slab(lay.k_a, h), gka_ref[...]))
        ka_ref[0, :, cols(h)] = k
        ka16_ref[0, :, cols(h)] = k.astype(BF16)
        v = slab(lay.v_a, h)
        va_ref[0, :, cols(h)] = v
        va16_ref[0, :, cols(h)] = v.astype(BF16)
    for h in range(N_IDX_HEADS * di // LANES):
        qi16_ref[0, :, cols(h)] = rope_i(slab(lay.q_i, h)).astype(BF16)
    x = slab(lay.kiw, 0)
    lane = lax.broadcasted_iota(jnp.int32, (ts, LANES), 1)
    is_k = lane < di
    ms = jnp.sum(jnp.where(is_k, x * x, 0.0), axis=-1, keepdims=True) * (1.0 / di)
    ki = rope_i(x * lax.rsqrt(ms + EPS) * gki_ref[...])
    ki_ref[0] = ki[:, :di]
    kiw = jnp.where(is_k, ki, x)
    kiw_ref[0] = kiw
    kiw16_ref[0] = kiw.astype(BF16)
    for h in range(2 * lay.hb):
        q = rope_h(hnorm(slab(lay.q_b, h), gqb_ref[...]))
        qb16_ref[0, :, cols(h)] = (q * scale).astype(BF16)
        k = rope_h(hnorm(slab(lay.k_b, h), gkb_ref[...]))
        kb_ref[0, :, cols(h)] = k
        kb16_ref[0, :, cols(h)] = k.astype(BF16)
        v = slab(lay.v_b, h)
        vb_ref[0, :, cols(h)] = v
        vb16_ref[0, :, cols(h)] = v.astype(BF16)


def _post(z, lay, gains, tables, pos_off):
    B, S, W = z.shape
    hd, di = lay.hd, lay.di
    ts = _pick(S, (128, 64, 32, 16, 8))
    wa, wkv, wqi, wb = lay.ha * hd, lay.kv * hd, N_IDX_HEADS * di, lay.hb * 2 * hd

    def rows(width):
        return pl.BlockSpec((1, ts, width), lambda b, s: (b, s, 0))

    def const(a):
        return pl.BlockSpec(a.shape, lambda b, s: (0,) * a.ndim)

    out_widths = [(wkv, F32), (wkv, F32), (di, F32), (wb, F32), (wb, F32),
                  (wa, BF16), (wkv, BF16), (wkv, BF16), (wqi, BF16), (LANES, F32), (LANES, BF16),
                  (wb, BF16), (wb, BF16), (wb, BF16)]
    return pl.pallas_call(
        functools.partial(_post_kernel, lay=lay, ts=ts, pos_off=pos_off,
                          scale=hd ** -0.5 * math.log2(math.e)),
        grid=(B, S // ts),
        in_specs=[rows(W)] + [const(g) for g in gains] + [const(t) for t in tables],
        out_specs=[rows(w) for w, _ in out_widths],
        out_shape=[jax.ShapeDtypeStruct((B, S, w), dt) for w, dt in out_widths],
        compiler_params=_cparams(("parallel", "parallel")),
        name="head_norm_rope",
    )(z, *gains, *tables)


def _tile_bounds(p0, tq, tk, n_keys):
    a_full = jnp.minimum((p0 // CHUNK + 1) * CHUNK, n_keys)
    a_tot = jnp.minimum(((p0 + tq - 1) // CHUNK + 1) * CHUNK, n_keys)
    return a_full // tk, (a_tot + tk - 1) // tk


def _allowed(j, p0, tq, tk, n_keys):
    shift = CHUNK.bit_length() - 1
    kpos = j * tk + lax.broadcasted_iota(jnp.int32, (tq, tk), 1)
    qpos = p0 + lax.broadcasted_iota(jnp.int32, (tq, tk), 0)
    return (lax.shift_right_logical(kpos, shift) <= lax.shift_right_logical(qpos, shift)) & (kpos < n_keys)


def _lane_tile_list(parts):
    return parts[0] if len(parts) == 1 else jnp.concatenate(parts, axis=1)


def _lane_tile(x, reps):
    return _lane_tile_list([x] * reps)


def _key_tile(lp):
    return 1024 if lp % 1024 == 0 else 256


def _lane_tile_max(s):
    out = s[:, :LANES]
    for c in range(1, s.shape[1] // LANES):
        out = jnp.maximum(out, s[:, c * LANES:(c + 1) * LANES])
    return out


def _softmax_step(s, v, m_ref, l_ref, acc_ref, idx, tile_max=None):
    tk = s.shape[1]
    wv = acc_ref.shape[-1]
    m_old = m_ref[idx]
    if tile_max is None:
        tile_max = _lane_tile_max(s)
    m_new = jnp.maximum(m_old, jnp.max(tile_max, axis=1, keepdims=True))
    alpha = jnp.exp2(m_old - m_new)
    p = jnp.exp2(s - _lane_tile(m_new, tk // LANES))
    if l_ref is not None:
        l_ref[idx] = alpha * l_ref[idx] + jnp.sum(p, axis=1, keepdims=True)
    acc_ref[idx] = _lane_tile(alpha, wv // LANES) * acc_ref[idx] + jnp.dot(
        p.astype(BF16), v, preferred_element_type=F32)
    m_ref[idx] = m_new


def _dsa_kernel(qa_ref, qi_ref, wq_ref, ka_ref, va_ref, ki_ref, rank_ref, o_ref,
                keys_ref, wexp_ref, qih_ref, tau_ref, quota_ref, seen_ref, bias_ref, m_ref, acc_ref, s_ref, mx_ref,
                *, tq, tk, n_keys, pos_off, topk, ha, kv, hd, di):
    p0 = pos_off + pl.program_id(1) * tq
    n_full, n_tot = _tile_bounds(p0, tq, tk, n_keys)
    reps = tk // LANES
    group = ha // kv
    int_min = jnp.int32(-2 ** 31)

    def rows(r):
        return slice(r * tq, (r + 1) * tq)

    lane = lax.broadcasted_iota(jnp.int32, (tq, LANES), 1)
    w_scale = (N_IDX_HEADS ** -0.5) * (di ** -0.5)
    for h in range(N_IDX_HEADS):
        pair = qi_ref[0, :, (h * di // LANES) * LANES:(h * di // LANES + 1) * LANES].astype(F32)
        off = (h * di) % LANES
        if off:
            pair = pltpu.roll(pair, LANES - off, 1)
        qih_ref[h // IDX_STACK, rows(h % IDX_STACK), :] = jnp.where(lane < di, pair, 0.0).astype(BF16)
        wexp_ref[h] = jnp.broadcast_to(wq_ref[0, :, di + h:di + h + 1] * w_scale, (tq, LANES))

    def key_slice(j):
        return pl.ds(pl.multiple_of(j * tk, tk), tk)

    def score_tile(j, masked):
        kt = ki_ref[0, key_slice(j), :]
        parts = [jnp.zeros((tq, LANES), F32)] * reps
        for hs in range(N_IDX_HEADS // IDX_STACK):
            lg = lax.dot_general(qih_ref[hs], kt, NT_DIMS, preferred_element_type=F32)
            for r in range(IDX_STACK):
                w = wexp_ref[hs * IDX_STACK + r]
                parts = [parts[c] + jnp.maximum(lg[rows(r), c * LANES:(c + 1) * LANES], 0.0) * w
                         for c in range(reps)]
        sc = _lane_tile_list(parts)
        if masked:
            sc = jnp.where(_allowed(j, p0, tq, tk, n_keys), sc, -jnp.inf)
        bits = pltpu.bitcast(sc, jnp.int32)
        keys_ref[j] = jnp.where(bits < 0, bits ^ jnp.int32(0x7FFFFFFF), bits)

    def score_full(j, c):
        score_tile(j, False)
        return c

    def score_masked(j, c):
        score_tile(j, True)
        return c

    lax.fori_loop(0, n_full, score_full, 0)
    lax.fori_loop(n_full, n_tot, score_masked, 0)

    def count_ge(cand):
        def body(j, acc):
            t = keys_ref[j]
            for c in range(reps):
                acc = jnp.where(t[:, c * LANES:(c + 1) * LANES] >= cand, acc + 1.0, acc)
            return acc
        acc = lax.fori_loop(0, n_tot, body, jnp.zeros((tq, LANES), F32))
        return jnp.sum(acc, axis=1, keepdims=True)

    kf = float(topk)
    zero = jnp.zeros((tq, LANES), jnp.int32)
    c0 = jnp.broadcast_to(count_ge(zero), (tq, LANES))
    n_all = jnp.broadcast_to((n_tot * tk).astype(F32), (tq, LANES))
    state = (jnp.int32(0), jnp.where(c0 >= kf, zero, int_min), jnp.where(c0 >= kf, c0, n_all))

    def bit_cond(state):
        b, _, cnt = state
        return (b < 31) & (jnp.max(jnp.where(cnt == kf, 0.0, 1.0)) > 0.0)

    def bit_body(state):
        b, prefix, cnt = state
        cand = prefix + lax.shift_left(jnp.int32(1), 30 - b)
        c = jnp.broadcast_to(count_ge(cand), (tq, LANES))
        return b + 1, jnp.where(c >= kf, cand, prefix), jnp.where(c >= kf, c, cnt)

    tau = lax.while_loop(bit_cond, bit_body, state)[1]
    tau_ref[...] = tau
    quota_ref[...] = kf - jnp.broadcast_to(count_ge(tau + 1), (tq, LANES))
    seen_ref[...] = jnp.zeros((tq, LANES), F32)

    m_ref[...] = jnp.full(m_ref.shape, NEG, F32)
    acc_ref[...] = jnp.zeros(acc_ref.shape, F32)
    ones = jnp.ones((tk, hd), BF16)

    def attn_tile(j, masked):
        key = keys_ref[j]
        tau_t = tau_ref[...]
        quota = quota_ref[...]
        seen = seen_ref[...]
        parts = []
        for c in range(reps):
            kc = key[:, c * LANES:(c + 1) * LANES]
            tie = kc == tau_t
            cnt = jnp.dot(jnp.where(tie, 1.0, 0.0).astype(BF16), rank_ref[...], preferred_element_type=F32)
            sel = (kc > tau_t) | (tie & (seen + cnt[:, :LANES] < quota))
            parts.append(jnp.where(sel, 0.0, NEG))
            seen = seen + cnt[:, LANES:]
        seen_ref[...] = seen
        bias = _lane_tile_list(parts)
        if masked:
            bias = jnp.where(_allowed(j, p0, tq, tk, n_keys), bias, NEG)
        bias_ref[...] = bias
        def produce(h):
            k = ka_ref[0, key_slice(j), (h // group) * hd:(h // group + 1) * hd]
            s = lax.dot_general(qa_ref[0, :, h * hd:(h + 1) * hd], k, NT_DIMS, preferred_element_type=F32)
            s = s + bias_ref[...]
            s_ref[h % 2] = s
            mx_ref[h % 2] = _lane_tile_max(s)

        def consume(h):
            g = h // group
            v1 = jnp.concatenate([va_ref[0, key_slice(j), g * hd:(g + 1) * hd], ones], axis=1)
            _softmax_step(s_ref[h % 2], v1, m_ref, None, acc_ref, h, tile_max=mx_ref[h % 2])

        produce(0)
        for h in range(ha):
            if h + 1 < ha:
                produce(h + 1)
            consume(h)

    def attn_full(j, c):
        attn_tile(j, False)
        return c

    def attn_masked(j, c):
        attn_tile(j, True)
        return c

    lax.fori_loop(0, n_full, attn_full, 0)
    lax.fori_loop(n_full, n_tot, attn_masked, 0)
    for h in range(ha):
        o_ref[0, :, h * hd:(h + 1) * hd] = (acc_ref[h, :, :hd] / acc_ref[h, :, hd:]).astype(o_ref.dtype)


def _dsa(qa, qi, kiw_q, ka, va, kiw_k, n_keys, pos_off, kv, hd, di):
    B, S, wa = qa.shape
    lp = ka.shape[1]
    ha = wa // hd
    tq = _pick(S, (128, 64, 32, 16, 8))
    tk = _key_tile(lp)
    topk = min(TOPK_MAX, n_keys // 4)
    kern = functools.partial(_dsa_kernel, tq=tq, tk=tk, n_keys=n_keys, pos_off=pos_off, topk=topk,
                             ha=ha, kv=kv, hd=hd, di=di)

    def rows(width):
        return pl.BlockSpec((1, tq, width), lambda b, i: (b, i, 0))

    def resident(width):
        return pl.BlockSpec((1, lp, width), lambda b, i: (b, 0, 0), pipeline_mode=pl.Buffered(1))

    lane = np.arange(LANES)
    rank_mat = jnp.asarray(np.concatenate([(lane[:, None] < lane[None, :]).astype(np.float32),
                                           np.ones((LANES, LANES), np.float32)], axis=1), dtype=BF16)
    return pl.pallas_call(
        kern,
        grid=(B, S // tq),
        in_specs=[rows(wa), rows(qi.shape[2]), rows(LANES),
                  resident(kv * hd), resident(kv * hd), resident(LANES),
                  pl.BlockSpec((LANES, 2 * LANES), lambda b, i: (0, 0))],
        out_specs=rows(wa),
        out_shape=jax.ShapeDtypeStruct((B, S, wa), BF16),
        scratch_shapes=[pltpu.VMEM((lp // tk, tq, tk), jnp.int32),
                        pltpu.VMEM((N_IDX_HEADS, tq, LANES), F32),
                        pltpu.VMEM((N_IDX_HEADS // IDX_STACK, IDX_STACK * tq, LANES), BF16),
                        pltpu.VMEM((tq, LANES), jnp.int32),
                        pltpu.VMEM((tq, LANES), F32),
                        pltpu.VMEM((tq, LANES), F32),
                        pltpu.VMEM((tq, tk), F32),
                        pltpu.VMEM((ha, tq, LANES), F32),
                        pltpu.VMEM((ha, tq, 2 * hd), F32),
                        pltpu.VMEM((2, tq, tk), F32),
                        pltpu.VMEM((2, tq, LANES), F32)],
        compiler_params=_cparams(("parallel", "arbitrary")),
        name="dsa_attention",
    )(qa, qi, kiw_q, ka, va, kiw_k, rank_mat)


def _diff_kernel(q_ref, k_ref, v_ref, lq1_ref, lk1_ref, lq2_ref, lk2_ref, sg_ref, o_ref,
                 m_ref, l_ref, acc_ref, s_ref, mx_ref, *, tq, tk, n_keys, pos_off, hd, lam_init):
    p0 = pos_off + pl.program_id(2) * tq
    n_full, n_tot = _tile_bounds(p0, tq, tk, n_keys)
    db = 2 * hd
    m_ref[...] = jnp.full(m_ref.shape, NEG, F32)
    l_ref[...] = jnp.zeros(l_ref.shape, F32)
    acc_ref[...] = jnp.zeros(acc_ref.shape, F32)

    def key_slice(j):
        return pl.ds(pl.multiple_of(j * tk, tk), tk)

    def scores(j, c):
        return lax.dot_general(q_ref[0, :, c * hd:(c + 1) * hd], k_ref[0, key_slice(j), c * hd:(c + 1) * hd],
                               NT_DIMS, preferred_element_type=F32)

    def produce(j, slot, masked=False):
        ok = _allowed(j, p0, tq, tk, n_keys) if masked else None
        for c in range(2):
            s = scores(j, c)
            if masked:
                s = jnp.where(ok, s, NEG)
            s_ref[slot, c] = s
            mx_ref[slot, c] = _lane_tile_max(s)

    def consume(j, slot):
        v = v_ref[0, key_slice(j), :]
        for c in range(2):
            _softmax_step(s_ref[slot, c], v, m_ref, l_ref, acc_ref, c, tile_max=mx_ref[slot, c])

    n_pipe = jnp.minimum(n_full + 1, n_tot)
    n_pairs = jnp.maximum(n_pipe - 2, 0) // 2
    first = 2 * n_pairs
    left = n_pipe - first

    @pl.when(n_pipe >= 2)
    def _():
        produce(0, 0)

    def pair_body(t, carry):
        produce(2 * t + 1, 1)
        consume(2 * t, 0)
        produce(2 * t + 2, 0)
        consume(2 * t + 1, 1)
        return carry

    lax.fori_loop(0, n_pairs, pair_body, 0)

    @pl.when(left == 1)
    def _():
        produce(first, 0, masked=True)
        consume(first, 0)

    @pl.when(left == 2)
    def _():
        produce(first + 1, 1, masked=True)
        consume(first, 0)
        consume(first + 1, 1)

    @pl.when(left == 3)
    def _():
        produce(first + 1, 1)
        consume(first, 0)
        produce(first + 2, 0, masked=True)
        consume(first + 1, 1)
        consume(first + 2, 0)

    def tile_masked(j, carry):
        produce(j, 0, masked=True)
        consume(j, 0)
        return carry

    lax.fori_loop(n_pipe, n_tot, tile_masked, 0)

    lam = (jnp.exp(jnp.sum(lq1_ref[...] * lk1_ref[...], axis=-1, keepdims=True))
           - jnp.exp(jnp.sum(lq2_ref[...] * lk2_ref[...], axis=-1, keepdims=True)) + lam_init)
    reps = db // LANES
    o = acc_ref[0] / _lane_tile(l_ref[0], reps) - lam * (acc_ref[1] / _lane_tile(l_ref[1], reps))
    ms = jnp.mean(o * o, axis=-1, keepdims=True)
    o_ref[0] = (o * lax.rsqrt(ms + EPS) * sg_ref[...] * (1.0 - lam_init)).astype(o_ref.dtype)


def _diff(qb, kb, vb, lams, subln_g, n_keys, pos_off, hd, lam_init):
    B, S, wb = qb.shape
    lp = kb.shape[1]
    db = 2 * hd
    hb = wb // db
    tq = _pick(S, (512, 256, 128, 64, 32, 16, 8))
    tk = _key_tile(lp)
    kern = functools.partial(_diff_kernel, tq=tq, tk=tk, n_keys=n_keys, pos_off=pos_off, hd=hd,
                             lam_init=lam_init)
    vec = pl.BlockSpec((1, hd), lambda b, h, i: (0, 0))
    return pl.pallas_call(
        kern,
        grid=(B, hb, S // tq),
        in_specs=[pl.BlockSpec((1, tq, db), lambda b, h, i: (b, i, h)),
                  pl.BlockSpec((1, lp, db), lambda b, h, i: (b, 0, h)),
                  pl.BlockSpec((1, lp, db), lambda b, h, i: (b, 0, h)),
                  vec, vec, vec, vec,
                  pl.BlockSpec((1, db), lambda b, h, i: (0, 0))],
        out_specs=pl.BlockSpec((1, tq, db), lambda b, h, i: (b, i, h)),
        out_shape=jax.ShapeDtypeStruct((B, S, wb), BF16),
        scratch_shapes=[pltpu.VMEM((2, tq, LANES), F32),
                        pltpu.VMEM((2, tq, LANES), F32),
                        pltpu.VMEM((2, tq, db), F32),
                        pltpu.VMEM((2, 2, tq, tk), F32),
                        pltpu.VMEM((2, 2, tq, LANES), F32)],
        compiler_params=_cparams(("parallel", "parallel", "arbitrary")),
        name="diff_attention",
    )(qb, kb, vb, *lams, subln_g)


def _pad_rows(a, lp):
    return jnp.pad(a, ((0, 0), (0, lp - a.shape[1]), (0, 0)))


def _prep_w_in(w_in, lay, d_idx_w):
    ha, kv, hb, hd, di = lay.ha, lay.kv, lay.hb, lay.hd, lay.di
    widths = (ha * hd, kv * hd, kv * hd, N_IDX_HEADS * di, di, d_idx_w, hb * 2 * hd, hb * 2 * hd, hb * 2 * hd)
    offs = np.cumsum((0,) + widths)
    seg = [w_in[:, offs[i]:offs[i + 1]] for i in range(len(widths))]
    d = w_in.shape[0]
    pad_kiw = jnp.zeros((d, LANES - di - d_idx_w), w_in.dtype)
    pad_end = jnp.zeros((d, lay.width - lay.end), w_in.dtype)
    return jnp.concatenate(seg[:6] + [pad_kiw] + seg[6:] + [pad_end], axis=1).astype(BF16)


def _layer(x, B, S, mods, past, pos_off, lw, li):
    (sh1, sc1, gt1, sh2, sc2, gt2, sh3, sc3, gt3) = mods
    D = x.shape[1]
    lay = lw["lay"]
    hd, di, kv = lay.hd, lay.di, lay.kv

    h = _norm_mod(x, lw["norm_ffn1"], sc1, sh1)
    a = _mm_up(h, lw["w1_1"], lw["w3_1"])
    x = _mm_res([a], [lw["w2_1"]], x, gt1, 0.5, "ffn_down")

    h = _norm_mod(x, lw["norm_attn"], sc2, sh2)
    z = _mm_plain(h, lw["w_in"]).reshape(B, S, lay.width)
    (k_a, v_a, k_i, k_b, v_b, qa16, ka16, va16, qi16, kiw, kiw16, qb16, kb16, vb16) = _post(
        z, lay, lw["gains"], lw["tables"], pos_off)

    if past is not None:
        ck_a, cv_a, ck_i, ck_b, cv_b = past
        P = ck_a.shape[1]
        flat = lambda c: c.reshape(B, P, -1).astype(BF16)
        ka16 = jnp.concatenate([flat(ck_a), ka16], axis=1)
        va16 = jnp.concatenate([flat(cv_a), va16], axis=1)
        ck_i16 = jnp.pad(flat(ck_i), ((0, 0), (0, 0), (0, LANES - di)))
        kiw16_k = jnp.concatenate([ck_i16, kiw16], axis=1)
        kb16 = jnp.concatenate([flat(ck_b), kb16], axis=1)
        vb16 = jnp.concatenate([flat(cv_b), vb16], axis=1)
        n_keys = P + S
    else:
        kiw16_k = kiw16
        n_keys = S
    lp = _round_up(n_keys, 256)
    if lp != n_keys:
        ka16, va16, kiw16_k, kb16, vb16 = (_pad_rows(t, lp) for t in (ka16, va16, kiw16_k, kb16, vb16))

    oa = _dsa(qa16, qi16, kiw, ka16, va16, kiw16_k, n_keys, pos_off, kv, hd, di)
    ob = _diff(qb16, kb16, vb16, lw["lams"], lw["subln_g"], n_keys, pos_off, hd, _lambda_init(li))
    wa = oa.shape[2]
    x = _mm_res([oa.reshape(B * S, wa), ob.reshape(B * S, -1)], [lw["w_out"][:wa], lw["w_out"][wa:]],
                x, gt2, 1.0, "out_proj")

    h = _norm_mod(x, lw["norm_ffn2"], sc3, sh3)
    a = _mm_up(h, lw["w1_2"], lw["w3_2"])
    x = _mm_res([a], [lw["w2_2"]], x, gt3, 0.5, "ffn_down")
    return x, (k_a, v_a, k_i, k_b, v_b)


def kernel(x_prompt, x_sample, cache_k_a, cache_v_a, cache_k_idx, cache_k_b, cache_v_b, c_prompt, c_sample, w_ada, b_ada, norm_ffn1, w1_ffn1, w3_ffn1, w2_ffn1, norm_attn, w_in, qnorm_a, knorm_a, knorm_idx, qnorm_b, knorm_b, lam_q1, lam_k1, lam_q2, lam_k2, subln_b, w_out, norm_ffn2, w1_ffn2, w3_ffn2, w2_ffn2):
    depth = w_ada.shape[0]
    Bp, Sp, D = x_prompt.shape
    Bs, Ss, _ = x_sample.shape
    P = cache_k_a.shape[2]
    kv, hd = cache_k_a.shape[3], cache_k_a.shape[4]
    di = cache_k_idx.shape[3]
    hb = cache_k_b.shape[3]
    db = cache_v_b.shape[4]
    ha = (w_out.shape[1] - hb * db) // hd
    lay = _Layout(ha, kv, hb, hd, di)
    tables = _rope_tables(hd, di)

    xp = x_prompt.reshape(Bp * Sp, D)
    xs = x_sample.reshape(Bs * Ss, D)
    n_c = Bp + Bs
    c_all = jnp.pad(jnp.concatenate([c_prompt, c_sample], axis=0), ((0, _round_up(n_c, 8) - n_c), (0, 0)))

    rows_p, rows_s = [], []
    for l in range(depth):
        row = lambda a: a[l][None, :]
        gki = jnp.pad(knorm_idx[l], (0, LANES - di))[None, :]
        lw = dict(
            lay=lay, tables=tables,
            norm_ffn1=row(norm_ffn1), norm_attn=row(norm_attn), norm_ffn2=row(norm_ffn2),
            w1_1=w1_ffn1[l], w3_1=w3_ffn1[l], w2_1=w2_ffn1[l].astype(BF16),
            w1_2=w1_ffn2[l], w3_2=w3_ffn2[l], w2_2=w2_ffn2[l].astype(BF16),
            w_in=_prep_w_in(w_in[l], lay, N_IDX_HEADS), w_out=w_out[l].astype(BF16),
            gains=(row(qnorm_a), row(knorm_a), gki, row(qnorm_b), row(knorm_b)),
            lams=(row(lam_q1), row(lam_k1), row(lam_q2), row(lam_k2)),
            subln_g=row(subln_b),
        )
        mod = _ada(c_all, w_ada[l], b_ada[l][None, :])

        def mods_for(lo, nb, s):
            m = mod[lo:lo + nb].reshape(nb, N_MOD, D)
            if nb == 1:
                return tuple(m[:, i] for i in range(N_MOD))
            return tuple(jnp.repeat(m[:, i], s, axis=0) for i in range(N_MOD))

        xp, rp = _layer(xp, Bp, Sp, mods_for(0, Bp, Sp), None, 0, lw, l)
        past = (cache_k_a[l], cache_v_a[l], cache_k_idx[l], cache_k_b[l], cache_v_b[l])
        xs, rs = _layer(xs, Bs, Ss, mods_for(Bp, Bs, Ss), past, P, lw, l)
        rows_p.append(rp)
        rows_s.append(rs)

    def stack(rows, idx, B, S, tail):
        return jnp.stack([r[idx].reshape((B, S) + tail) for r in rows])

    tails = ((kv, hd), (kv, hd), (di,), (hb, 2, hd), (hb, db))
    outs_p = tuple(stack(rows_p, i, Bp, Sp, tails[i]) for i in range(5))
    outs_s = tuple(stack(rows_s, i, Bs, Ss, tails[i]) for i in range(5))
    return (xp.reshape(Bp, Sp, D), xs.reshape(Bs, Ss, D)) + outs_p + outs_s
```

```python
import functools
import math

import numpy as np
import jax
import jax.numpy as jnp
from jax import lax
from jax.experimental import pallas as pl
from jax.experimental.pallas import tpu as pltpu

F32 = jnp.float32
BF16 = jnp.bfloat16

CHUNK = 64
N_IDX_HEADS = 16
IDX_STACK = 4
TOPK_MAX = 256
ROPE_THETA = 500000.0
ROT_FRAC = 4
EPS = 1e-6
N_MOD = 9

LANES = 128
V7X_VMEM_BYTES = 64 * 1024 * 1024
VMEM_LIMIT = V7X_VMEM_BYTES * 7 // 8
NEG = -1e30
NT_DIMS = (((1,), (1,)), ((), ()))


def _lambda_init(layer_idx):
    return 0.8 - 0.6 * math.exp(-0.3 * layer_idx)


def _pick(n, candidates):
    for c in candidates:
        if n % c == 0:
            return c
    return n


def _cparams(semantics):
    return pltpu.CompilerParams(dimension_semantics=semantics, vmem_limit_bytes=VMEM_LIMIT)


def _round_up(n, m):
    return (n + m - 1) // m * m


def _ada_kernel(c_ref, w_ref, b_ref, o_ref):
    c = c_ref[...]
    s = (c * jax.nn.sigmoid(c)).astype(BF16)
    o_ref[...] = jnp.dot(s, w_ref[...].astype(BF16), preferred_element_type=F32) + b_ref[...]


def _ada(c, w, b):
    R, D = c.shape
    N = w.shape[1]
    tn = _pick(N, (512, 256, 128))
    return pl.pallas_call(
        _ada_kernel,
        grid=(N // tn,),
        in_specs=[pl.BlockSpec((R, D), lambda j: (0, 0)),
                  pl.BlockSpec((D, tn), lambda j: (0, j)),
                  pl.BlockSpec((1, tn), lambda j: (0, j))],
        out_specs=pl.BlockSpec((R, tn), lambda j: (0, j)),
        out_shape=jax.ShapeDtypeStruct((R, N), F32),
        compiler_params=_cparams(("parallel",)),
        name="ada_matvec",
    )(c, w, b)


def _norm_mod_kernel(x_ref, g_ref, sc_ref, sh_ref, o_ref):
    x = x_ref[...]
    ms = jnp.mean(x * x, axis=-1, keepdims=True)
    y = x * lax.rsqrt(ms + EPS) * g_ref[...]
    o_ref[...] = (y * (1.0 + sc_ref[...]) + sh_ref[...]).astype(o_ref.dtype)


def _norm_mod(x, g, scale, shift):
    M, D = x.shape
    tm = _pick(M, (512, 256, 128, 64, 32, 16, 8))

    def mspec(a):
        if a.shape[0] == 1:
            return pl.BlockSpec((1, D), lambda i: (0, 0))
        return pl.BlockSpec((tm, D), lambda i: (i, 0))

    return pl.pallas_call(
        _norm_mod_kernel,
        grid=(M // tm,),
        in_specs=[pl.BlockSpec((tm, D), lambda i: (i, 0)),
                  pl.BlockSpec((1, D), lambda i: (0, 0)),
                  mspec(scale), mspec(shift)],
        out_specs=pl.BlockSpec((tm, D), lambda i: (i, 0)),
        out_shape=jax.ShapeDtypeStruct((M, D), BF16),
        compiler_params=_cparams(("parallel",)),
        name="norm_mod",
    )(x, g, scale, shift)


def _mm_up_kernel(h_ref, w1_ref, w3_ref, o_ref):
    h = h_ref[...]
    u = jnp.dot(h, w1_ref[...].astype(BF16), preferred_element_type=F32)
    v = jnp.dot(h, w3_ref[...].astype(BF16), preferred_element_type=F32)
    o_ref[...] = (u * jax.nn.sigmoid(u) * v).astype(o_ref.dtype)


def _mm_up(h, w1, w3):
    M, K = h.shape
    F = w1.shape[1]
    tm = _pick(M, (2048, 1024, 512, 256, 128, 64, 32, 16, 8))
    tn = _pick(F, (256, 128))
    return pl.pallas_call(
        _mm_up_kernel,
        grid=(M // tm, F // tn),
        in_specs=[pl.BlockSpec((tm, K), lambda i, j: (i, 0), pipeline_mode=pl.Buffered(1)),
                  pl.BlockSpec((K, tn), lambda i, j: (0, j)),
                  pl.BlockSpec((K, tn), lambda i, j: (0, j))],
        out_specs=pl.BlockSpec((tm, tn), lambda i, j: (i, j)),
        out_shape=jax.ShapeDtypeStruct((M, F), BF16),
        compiler_params=_cparams(("parallel", "parallel")),
        name="ffn_up",
    )(h, w1, w3)


def _mm_plain_kernel(h_ref, w_ref, o_ref):
    o_ref[...] = jnp.dot(h_ref[...], w_ref[...], preferred_element_type=F32)


def _mm_plain(h, w):
    M, K = h.shape
    N = w.shape[1]
    tm = _pick(M, (1024, 512, 256, 128, 64, 32, 16, 8))
    tn = _pick(N, (512, 256, 128))
    return pl.pallas_call(
        _mm_plain_kernel,
        grid=(M // tm, N // tn),
        in_specs=[pl.BlockSpec((tm, K), lambda i, j: (i, 0)),
                  pl.BlockSpec((K, tn), lambda i, j: (0, j))],
        out_specs=pl.BlockSpec((tm, tn), lambda i, j: (i, j)),
        out_shape=jax.ShapeDtypeStruct((M, N), F32),
        compiler_params=_cparams(("parallel", "parallel")),
        name="in_proj",
    )(h, w)


def _mm_res_kernel(*refs, n_lhs, coef):
    a_refs = refs[:n_lhs]
    w_refs = refs[n_lhs:2 * n_lhs]
    x_ref, g_ref, o_ref = refs[2 * n_lhs:]
    y = jnp.dot(a_refs[0][...], w_refs[0][...], preferred_element_type=F32)
    for a_ref, w_ref in zip(a_refs[1:], w_refs[1:]):
        y = y + jnp.dot(a_ref[...], w_ref[...], preferred_element_type=F32)
    o_ref[...] = x_ref[...] + (coef * g_ref[...]) * y


def _mm_res(a_list, w_list, x, gate, coef, name):
    M, N = x.shape
    ktot = sum(a.shape[1] for a in a_list)
    tm = _pick(M, (1024, 512, 256, 128, 64, 32, 16, 8))
    tn = _pick(N, (512, 256, 128)) if ktot <= 4096 else _pick(N, (256, 128))
    lhs_mode = {} if ktot <= 4096 else dict(pipeline_mode=pl.Buffered(1))
    in_specs = [pl.BlockSpec((tm, a.shape[1]), lambda i, j: (i, 0), **lhs_mode) for a in a_list]
    in_specs += [pl.BlockSpec((w.shape[0], tn), lambda i, j: (0, j)) for w in w_list]
    in_specs.append(pl.BlockSpec((tm, tn), lambda i, j: (i, j)))
    if gate.shape[0] == 1:
        in_specs.append(pl.BlockSpec((1, tn), lambda i, j: (0, j)))
    else:
        in_specs.append(pl.BlockSpec((tm, tn), lambda i, j: (i, j)))
    return pl.pallas_call(
        functools.partial(_mm_res_kernel, n_lhs=len(a_list), coef=coef),
        grid=(M // tm, N // tn),
        in_specs=in_specs,
        out_specs=pl.BlockSpec((tm, tn), lambda i, j: (i, j)),
        out_shape=jax.ShapeDtypeStruct((M, N), F32),
        compiler_params=_cparams(("parallel", "parallel")),
        name=name,
    )(*a_list, *w_list, x, gate)


class _Layout:
    def __init__(self, ha, kv, hb, hd, di):
        self.ha, self.kv, self.hb, self.hd, self.di = ha, kv, hb, hd, di
        self.q_a = 0
        self.k_a = self.q_a + ha * hd
        self.v_a = self.k_a + kv * hd
        self.q_i = self.v_a + kv * hd
        self.kiw = self.q_i + N_IDX_HEADS * di
        self.q_b = self.kiw + LANES
        self.k_b = self.q_b + hb * 2 * hd
        self.v_b = self.k_b + hb * 2 * hd
        self.end = self.v_b + hb * 2 * hd
        self.width = _round_up(self.end, 512)


def _rope_tables(hd, di):
    def table(d):
        r = d // ROT_FRAC
        half = r // 2
        inv = ROPE_THETA ** (-(jnp.arange(half, dtype=jnp.float32) * 2.0 / r))
        lane = np.arange(LANES) % d
        idx = np.where(lane < half, lane, np.where(lane < r, lane - half, 0))
        invl = jnp.where(lane < r, inv[idx], 0.0)
        sign = np.where(lane < half, -1.0, np.where(lane < r, 1.0, 0.0)).astype(np.float32)
        src = np.where(lane < half, np.arange(LANES) + half, np.arange(LANES) - half)
        partner = np.zeros((LANES, LANES), np.float32)
        partner[src[lane < r], np.arange(LANES)[lane < r]] = 1.0
        return jnp.stack([invl, jnp.asarray(sign)])[:, None, :], jnp.asarray(partner, dtype=BF16)
    th, ph = table(hd)
    ti, pi = table(di)
    return th, ph, ti, pi


def _split3(x):
    x1 = x.astype(BF16)
    r1 = x - x1.astype(F32)
    x2 = r1.astype(BF16)
    x3 = (r1 - x2.astype(F32)).astype(BF16)
    return x1, x2, x3


def _post_kernel(z_ref, gqa_ref, gka_ref, gki_ref, gqb_ref, gkb_ref, th_ref, ph_ref, ti_ref, pi_ref,
                 ka_ref, va_ref, ki_ref, kb_ref, vb_ref,
                 qa16_ref, ka16_ref, va16_ref, qi16_ref, kiw_ref, kiw16_ref, qb16_ref, kb16_ref, vb16_ref,
                 *, lay, ts, pos_off, scale):
    hd, di = lay.hd, lay.di
    row = lax.broadcasted_iota(jnp.int32, (ts, LANES), 0)
    pos = (pos_off + pl.program_id(1) * ts + row).astype(F32)

    def rope_fn(t_ref, p_ref):
        ang = pos * t_ref[0]
        cosv = jnp.cos(ang)
        sinv = jnp.sin(ang) * t_ref[1]

        def rope(x):
            partner = sum(jnp.dot(t, p_ref[...], preferred_element_type=F32) for t in _split3(x))
            return x * cosv + partner * sinv
        return rope

    rope_h = rope_fn(th_ref, ph_ref)
    rope_i = rope_fn(ti_ref, pi_ref)

    def hnorm(x, g):
        ms = jnp.mean(x * x, axis=-1, keepdims=True)
        return x * lax.rsqrt(ms + EPS) * g

    def slab(off, h):
        return z_ref[0, :, off + h * hd: off + (h + 1) * hd]

    def cols(h):
        return slice(h * hd, (h + 1) * hd)

    for h in range(lay.ha):
        q = rope_h(hnorm(slab(lay.q_a, h), gqa_ref[...]))
        qa16_ref[0, :, cols(h)] = (q * scale).astype(BF16)
    for h in range(lay.kv):
        k = rope_h(hnorm(slab(lay.k_a, h), gka_ref[...]))
        ka_ref[0, :, cols(h)] = k
        ka16_ref[0, :, cols(h)] = k.astype(BF16)
        v = slab(lay.v_a, h)
        va_ref[0, :, cols(h)] = v
        va16_ref[0, :, cols(h)] = v.astype(BF16)
    for h in range(N_IDX_HEADS * di // LANES):
        qi16_ref[0, :, cols(h)] = rope_i(slab(lay.q_i, h)).astype(BF16)
    x = slab(lay.kiw, 0)
    lane = lax.broadcasted_iota(jnp.int32, (ts, LANES), 1)
    is_k = lane < di
    ms = jnp.sum(jnp.where(is_k, x * x, 0.0), axis=-1, keepdims=True) * (1.0 / di)
    ki = rope_i(x * lax.rsqrt(ms + EPS) * gki_ref[...])
    ki_ref[0] = ki[:, :di]
    kiw = jnp.where(is_k, ki, x)
    kiw_ref[0] = kiw
    kiw16_ref[0] = kiw.astype(BF16)
    for h in range(2 * lay.hb):
        q = rope_h(hnorm(slab(lay.q_b, h), gqb_ref[...]))
        qb16_ref[0, :, cols(h)] = (q * scale).astype(BF16)
        k = rope_h(hnorm(slab(lay.k_b, h), gkb_ref[...]))
        kb_ref[0, :, cols(h)] = k
        kb16_ref[0, :, cols(h)] = k.astype(BF16)
        v = slab(lay.v_b, h)
        vb_ref[0, :, cols(h)] = v
        vb16_ref[0, :, cols(h)] = v.astype(BF16)


def _post(z, lay, gains, tables, pos_off):
    B, S, W = z.shape
    hd, di = lay.hd, lay.di
    ts = _pick(S, (128, 64, 32, 16, 8))
    wa, wkv, wqi, wb = lay.ha * hd, lay.kv * hd, N_IDX_HEADS * di, lay.hb * 2 * hd

    def rows(width):
        return pl.BlockSpec((1, ts, width), lambda b, s: (b, s, 0))

    def const(a):
        return pl.BlockSpec(a.shape, lambda b, s: (0,) * a.ndim)

    out_widths = [(wkv, F32), (wkv, F32), (di, F32), (wb, F32), (wb, F32),
                  (wa, BF16), (wkv, BF16), (wkv, BF16), (wqi, BF16), (LANES, F32), (LANES, BF16),
                  (wb, BF16), (wb, BF16), (wb, BF16)]
    return pl.pallas_call(
        functools.partial(_post_kernel, lay=lay, ts=ts, pos_off=pos_off,
                          scale=hd ** -0.5 * math.log2(math.e)),
        grid=(B, S // ts),
        in_specs=[rows(W)] + [const(g) for g in gains] + [const(t) for t in tables],
        out_specs=[rows(w) for w, _ in out_widths],
        out_shape=[jax.ShapeDtypeStruct((B, S, w), dt) for w, dt in out_widths],
        compiler_params=_cparams(("parallel", "parallel")),
        name="head_norm_rope",
    )(z, *gains, *tables)


def _tile_bounds(p0, tq, tk, n_keys):
    a_full = jnp.minimum((p0 // CHUNK + 1) * CHUNK, n_keys)
    a_tot = jnp.minimum(((p0 + tq - 1) // CHUNK + 1) * CHUNK, n_keys)
    return a_full // tk, (a_tot + tk - 1) // tk


def _allowed(j, p0, tq, tk, n_keys):
    shift = CHUNK.bit_length() - 1
    kpos = j * tk + lax.broadcasted_iota(jnp.int32, (tq, tk), 1)
    qpos = p0 + lax.broadcasted_iota(jnp.int32, (tq, tk), 0)
    return (lax.shift_right_logical(kpos, shift) <= lax.shift_right_logical(qpos, shift)) & (kpos < n_keys)


def _lane_tile_list(parts):
    return parts[0] if len(parts) == 1 else jnp.concatenate(parts, axis=1)


def _lane_tile(x, reps):
    return _lane_tile_list([x] * reps)


KEY_TILE = 256
KEY_TILE_WIDE = 1024


def _key_tile(lp):
    return KEY_TILE_WIDE if lp % KEY_TILE_WIDE == 0 else KEY_TILE


def _lane_tile_max(s):
    out = s[:, :LANES]
    for c in range(1, s.shape[1] // LANES):
        out = jnp.maximum(out, s[:, c * LANES:(c + 1) * LANES])
    return out


def _softmax_step(s, v, m_ref, l_ref, acc_ref, idx, tile_max=None):
    tk = s.shape[1]
    wv = acc_ref.shape[-1]
    m_old = m_ref[idx]
    if tile_max is None:
        tile_max = _lane_tile_max(s)
    m_new = jnp.maximum(m_old, jnp.max(tile_max, axis=1, keepdims=True))
    alpha = jnp.exp2(m_old - m_new)
    p = jnp.exp2(s - _lane_tile(m_new, tk // LANES))
    if l_ref is not None:
        l_ref[idx] = alpha * l_ref[idx] + jnp.sum(p, axis=1, keepdims=True)
    acc_ref[idx] = _lane_tile(alpha, wv // LANES) * acc_ref[idx] + jnp.dot(
        p.astype(BF16), v, preferred_element_type=F32)
    m_ref[idx] = m_new


def _dsa_kernel(qa_ref, qi_ref, wq_ref, ka_ref, va_ref, ki_ref, rank_ref, o_ref,
                keys_ref, wexp_ref, qih_ref, tau_ref, quota_ref, seen_ref, bias_ref, m_ref, acc_ref, s_ref, mx_ref,
                *, tq, tk, n_keys, pos_off, topk, ha, kv, hd, di):
    p0 = pos_off + pl.program_id(1) * tq
    n_full, n_tot = _tile_bounds(p0, tq, tk, n_keys)
    reps = tk // LANES
    group = ha // kv
    int_min = jnp.int32(-2 ** 31)

    def rows(r):
        return slice(r * tq, (r + 1) * tq)

    lane = lax.broadcasted_iota(jnp.int32, (tq, LANES), 1)
    w_scale = (N_IDX_HEADS ** -0.5) * (di ** -0.5)
    for h in range(N_IDX_HEADS):
        pair = qi_ref[0, :, (h * di // LANES) * LANES:(h * di // LANES + 1) * LANES].astype(F32)
        off = (h * di) % LANES
        if off:
            pair = pltpu.roll(pair, LANES - off, 1)
        qih_ref[h // IDX_STACK, rows(h % IDX_STACK), :] = jnp.where(lane < di, pair, 0.0).astype(BF16)
        wexp_ref[h] = jnp.broadcast_to(wq_ref[0, :, di + h:di + h + 1] * w_scale, (tq, LANES))

    def key_slice(j):
        return pl.ds(pl.multiple_of(j * tk, tk), tk)

    def score_tile(j, masked):
        kt = ki_ref[0, key_slice(j), :]
        parts = [jnp.zeros((tq, LANES), F32)] * reps
        for hs in range(N_IDX_HEADS // IDX_STACK):
            lg = lax.dot_general(qih_ref[hs], kt, NT_DIMS, preferred_element_type=F32)
            for r in range(IDX_STACK):
                w = wexp_ref[hs * IDX_STACK + r]
                parts = [parts[c] + jnp.maximum(lg[rows(r), c * LANES:(c + 1) * LANES], 0.0) * w
                         for c in range(reps)]
        sc = _lane_tile_list(parts)
        if masked:
            sc = jnp.where(_allowed(j, p0, tq, tk, n_keys), sc, -jnp.inf)
        bits = pltpu.bitcast(sc, jnp.int32)
        keys_ref[j] = jnp.where(bits < 0, bits ^ jnp.int32(0x7FFFFFFF), bits)

    def score_full(j, c):
        score_tile(j, False)
        return c

    def score_masked(j, c):
        score_tile(j, True)
        return c

    lax.fori_loop(0, n_full, score_full, 0)
    lax.fori_loop(n_full, n_tot, score_masked, 0)

    def count_ge(cand):
        def body(j, acc):
            t = keys_ref[j]
            for c in range(reps):
                acc = jnp.where(t[:, c * LANES:(c + 1) * LANES] >= cand, acc + 1.0, acc)
            return acc
        acc = lax.fori_loop(0, n_tot, body, jnp.zeros((tq, LANES), F32))
        return jnp.sum(acc, axis=1, keepdims=True)

    kf = float(topk)
    zero = jnp.zeros((tq, LANES), jnp.int32)
    c0 = jnp.broadcast_to(count_ge(zero), (tq, LANES))
    n_all = jnp.broadcast_to((n_tot * tk).astype(F32), (tq, LANES))
    state = (jnp.int32(0), jnp.where(c0 >= kf, zero, int_min), jnp.where(c0 >= kf, c0, n_all))

    def bit_cond(state):
        b, _, cnt = state
        return (b < 31) & (jnp.max(jnp.where(cnt == kf, 0.0, 1.0)) > 0.0)

    def bit_body(state):
        b, prefix, cnt = state
        cand = prefix + lax.shift_left(jnp.int32(1), 30 - b)
        c = jnp.broadcast_to(count_ge(cand), (tq, LANES))
        return b + 1, jnp.where(c >= kf, cand, prefix), jnp.where(c >= kf, c, cnt)

    tau = lax.while_loop(bit_cond, bit_body, state)[1]
    tau_ref[...] = tau
    quota_ref[...] = kf - jnp.broadcast_to(count_ge(tau + 1), (tq, LANES))
    seen_ref[...] = jnp.zeros((tq, LANES), F32)

    m_ref[...] = jnp.full(m_ref.shape, NEG, F32)
    acc_ref[...] = jnp.zeros(acc_ref.shape, F32)
    ones = jnp.ones((tk, hd), BF16)

    def attn_tile(j, masked):
        key = keys_ref[j]
        tau_t = tau_ref[...]
        quota = quota_ref[...]
        seen = seen_ref[...]
        parts = []
        for c in range(reps):
            kc = key[:, c * LANES:(c + 1) * LANES]
            tie = kc == tau_t
            cnt = jnp.dot(jnp.where(tie, 1.0, 0.0).astype(BF16), rank_ref[...], preferred_element_type=F32)
            sel = (kc > tau_t) | (tie & (seen + cnt[:, :LANES] < quota))
            parts.append(jnp.where(sel, 0.0, NEG))
            seen = seen + cnt[:, LANES:]
        seen_ref[...] = seen
        bias = _lane_tile_list(parts)
        if masked:
            bias = jnp.where(_allowed(j, p0, tq, tk, n_keys), bias, NEG)
        bias_ref[...] = bias
        def produce(h):
            k = ka_ref[0, key_slice(j), (h // group) * hd:(h // group + 1) * hd]
            s = lax.dot_general(qa_ref[0, :, h * hd:(h + 1) * hd], k, NT_DIMS, preferred_element_type=F32)
            s = s + bias_ref[...]
            s_ref[h % 2] = s
            mx_ref[h % 2] = _lane_tile_max(s)

        def consume(h):
            g = h // group
            v1 = jnp.concatenate([va_ref[0, key_slice(j), g * hd:(g + 1) * hd], ones], axis=1)
            _softmax_step(s_ref[h % 2], v1, m_ref, None, acc_ref, h, tile_max=mx_ref[h % 2])

        produce(0)
        for h in range(ha):
            if h + 1 < ha:
                produce(h + 1)
            consume(h)

    def attn_full(j, c):
        attn_tile(j, False)
        return c

    def attn_masked(j, c):
        attn_tile(j, True)
        return c

    lax.fori_loop(0, n_full, attn_full, 0)
    lax.fori_loop(n_full, n_tot, attn_masked, 0)
    for h in range(ha):
        o_ref[0, :, h * hd:(h + 1) * hd] = (acc_ref[h, :, :hd] / acc_ref[h, :, hd:]).astype(o_ref.dtype)


def _dsa(qa, qi, kiw_q, ka, va, kiw_k, n_keys, pos_off, kv, hd, di):
    B, S, wa = qa.shape
    lp = ka.shape[1]
    ha = wa // hd
    tq = _pick(S, (128, 64, 32, 16, 8))
    tk = _key_tile(lp)
    topk = min(TOPK_MAX, n_keys // 4)
    kern = functools.partial(_dsa_kernel, tq=tq, tk=tk, n_keys=n_keys, pos_off=pos_off, topk=topk,
                             ha=ha, kv=kv, hd=hd, di=di)

    def rows(width):
        return pl.BlockSpec((1, tq, width), lambda b, i: (b, i, 0))

    def resident(width):
        return pl.BlockSpec((1, lp, width), lambda b, i: (b, 0, 0), pipeline_mode=pl.Buffered(1))

    lane = np.arange(LANES)
    rank_mat = jnp.asarray(np.concatenate([(lane[:, None] < lane[None, :]).astype(np.float32),
                                           np.ones((LANES, LANES), np.float32)], axis=1), dtype=BF16)
    return pl.pallas_call(
        kern,
        grid=(B, S // tq),
        in_specs=[rows(wa), rows(qi.shape[2]), rows(LANES),
                  resident(kv * hd), resident(kv * hd), resident(LANES),
                  pl.BlockSpec((LANES, 2 * LANES), lambda b, i: (0, 0))],
        out_specs=rows(wa),
        out_shape=jax.ShapeDtypeStruct((B, S, wa), BF16),
        scratch_shapes=[pltpu.VMEM((lp // tk, tq, tk), jnp.int32),
                        pltpu.VMEM((N_IDX_HEADS, tq, LANES), F32),
                        pltpu.VMEM((N_IDX_HEADS // IDX_STACK, IDX_STACK * tq, LANES), BF16),
                        pltpu.VMEM((tq, LANES), jnp.int32),
                        pltpu.VMEM((tq, LANES), F32),
                        pltpu.VMEM((tq, LANES), F32),
                        pltpu.VMEM((tq, tk), F32),
                        pltpu.VMEM((ha, tq, LANES), F32),
                        pltpu.VMEM((ha, tq, 2 * hd), F32),
                        pltpu.VMEM((2, tq, tk), F32),
                        pltpu.VMEM((2, tq, LANES), F32)],
        compiler_params=_cparams(("parallel", "arbitrary")),
        name="dsa_attention",
    )(qa, qi, kiw_q, ka, va, kiw_k, rank_mat)


def _diff_kernel(q_ref, k_ref, v_ref, lq1_ref, lk1_ref, lq2_ref, lk2_ref, sg_ref, o_ref,
                 m_ref, l_ref, acc_ref, s_ref, mx_ref, *, tq, tk, n_keys, pos_off, hd, lam_init):
    p0 = pos_off + pl.program_id(2) * tq
    n_full, n_tot = _tile_bounds(p0, tq, tk, n_keys)
    db = 2 * hd
    m_ref[...] = jnp.full(m_ref.shape, NEG, F32)
    l_ref[...] = jnp.zeros(l_ref.shape, F32)
    acc_ref[...] = jnp.zeros(acc_ref.shape, F32)

    def key_slice(j):
        return pl.ds(pl.multiple_of(j * tk, tk), tk)

    def scores(j, c):
        return lax.dot_general(q_ref[0, :, c * hd:(c + 1) * hd], k_ref[0, key_slice(j), c * hd:(c + 1) * hd],
                               NT_DIMS, preferred_element_type=F32)

    def produce(j, slot, masked=False):
        ok = _allowed(j, p0, tq, tk, n_keys) if masked else None
        for c in range(2):
            s = scores(j, c)
            if masked:
                s = jnp.where(ok, s, NEG)
            s_ref[slot, c] = s
            mx_ref[slot, c] = _lane_tile_max(s)

    def consume(j, slot):
        v = v_ref[0, key_slice(j), :]
        for c in range(2):
            _softmax_step(s_ref[slot, c], v, m_ref, l_ref, acc_ref, c, tile_max=mx_ref[slot, c])

    n_pipe = jnp.minimum(n_full + 1, n_tot)
    n_pairs = jnp.maximum(n_pipe - 2, 0) // 2
    first = 2 * n_pairs
    left = n_pipe - first

    @pl.when(n_pipe >= 2)
    def _():
        produce(0, 0)

    def pair_body(t, carry):
        produce(2 * t + 1, 1)
        consume(2 * t, 0)
        produce(2 * t + 2, 0)
        consume(2 * t + 1, 1)
        return carry

    lax.fori_loop(0, n_pairs, pair_body, 0)

    @pl.when(left == 1)
    def _():
        produce(first, 0, masked=True)
        consume(first, 0)

    @pl.when(left == 2)
    def _():
        produce(first + 1, 1, masked=True)
        consume(first, 0)
        consume(first + 1, 1)

    @pl.when(left == 3)
    def _():
        produce(first + 1, 1)
        consume(first, 0)
        produce(first + 2, 0, masked=True)
        consume(first + 1, 1)
        consume(first + 2, 0)

    def tile_masked(j, carry):
        produce(j, 0, masked=True)
        consume(j, 0)
        return carry

    lax.fori_loop(n_pipe, n_tot, tile_masked, 0)

    lam = (jnp.exp(jnp.sum(lq1_ref[...] * lk1_ref[...], axis=-1, keepdims=True))
           - jnp.exp(jnp.sum(lq2_ref[...] * lk2_ref[...], axis=-1, keepdims=True)) + lam_init)
    reps = db // LANES
    o = acc_ref[0] / _lane_tile(l_ref[0], reps) - lam * (acc_ref[1] / _lane_tile(l_ref[1], reps))
    ms = jnp.mean(o * o, axis=-1, keepdims=True)
    o_ref[0] = (o * lax.rsqrt(ms + EPS) * sg_ref[...] * (1.0 - lam_init)).astype(o_ref.dtype)


def _diff(qb, kb, vb, lams, subln_g, n_keys, pos_off, hd, lam_init):
    B, S, wb = qb.shape
    lp = kb.shape[1]
    db = 2 * hd
    hb = wb // db
    tq = _pick(S, (512, 256, 128, 64, 32, 16, 8))
    tk = _key_tile(lp)
    kern = functools.partial(_diff_kernel, tq=tq, tk=tk, n_keys=n_keys, pos_off=pos_off, hd=hd,
                             lam_init=lam_init)
    vec = pl.BlockSpec((1, hd), lambda b, h, i: (0, 0))
    return pl.pallas_call(
        kern,
        grid=(B, hb, S // tq),
        in_specs=[pl.BlockSpec((1, tq, db), lambda b, h, i: (b, i, h)),
                  pl.BlockSpec((1, lp, db), lambda b, h, i: (b, 0, h)),
                  pl.BlockSpec((1, lp, db), lambda b, h, i: (b, 0, h)),
                  vec, vec, vec, vec,
                  pl.BlockSpec((1, db), lambda b, h, i: (0, 0))],
        out_specs=pl.BlockSpec((1, tq, db), lambda b, h, i: (b, i, h)),
        out_shape=jax.ShapeDtypeStruct((B, S, wb), BF16),
        scratch_shapes=[pltpu.VMEM((2, tq, LANES), F32),
                        pltpu.VMEM((2, tq, LANES), F32),
                        pltpu.VMEM((2, tq, db), F32),
                        pltpu.VMEM((2, 2, tq, tk), F32),
                        pltpu.VMEM((2, 2, tq, LANES), F32)],
        compiler_params=_cparams(("parallel", "parallel", "arbitrary")),
        name="diff_attention",
    )(qb, kb, vb, *lams, subln_g)


def _pad_rows(a, lp):
    return jnp.pad(a, ((0, 0), (0, lp - a.shape[1]), (0, 0)))


def _prep_w_in(w_in, lay, d_idx_w):
    ha, kv, hb, hd, di = lay.ha, lay.kv, lay.hb, lay.hd, lay.di
    widths = (ha * hd, kv * hd, kv * hd, N_IDX_HEADS * di, di, d_idx_w, hb * 2 * hd, hb * 2 * hd, hb * 2 * hd)
    offs = np.cumsum((0,) + widths)
    seg = [w_in[:, offs[i]:offs[i + 1]] for i in range(len(widths))]
    d = w_in.shape[0]
    pad_kiw = jnp.zeros((d, LANES - di - d_idx_w), w_in.dtype)
    pad_end = jnp.zeros((d, lay.width - lay.end), w_in.dtype)
    return jnp.concatenate(seg[:6] + [pad_kiw] + seg[6:] + [pad_end], axis=1).astype(BF16)


def _layer(x, B, S, mods, past, pos_off, lw, li):
    (sh1, sc1, gt1, sh2, sc2, gt2, sh3, sc3, gt3) = mods
    D = x.shape[1]
    lay = lw["lay"]
    hd, di, kv = lay.hd, lay.di, lay.kv

    h = _norm_mod(x, lw["norm_ffn1"], sc1, sh1)
    a = _mm_up(h, lw["w1_1"], lw["w3_1"])
    x = _mm_res([a], [lw["w2_1"]], x, gt1, 0.5, "ffn_down")

    h = _norm_mod(x, lw["norm_attn"], sc2, sh2)
    z = _mm_plain(h, lw["w_in"]).reshape(B, S, lay.width)
    (k_a, v_a, k_i, k_b, v_b, qa16, ka16, va16, qi16, kiw, kiw16, qb16, kb16, vb16) = _post(
        z, lay, lw["gains"], lw["tables"], pos_off)

    if past is not None:
        ck_a, cv_a, ck_i, ck_b, cv_b = past
        P = ck_a.shape[1]
        flat = lambda c: c.reshape(B, P, -1).astype(BF16)
        ka16 = jnp.concatenate([flat(ck_a), ka16], axis=1)
        va16 = jnp.concatenate([flat(cv_a), va16], axis=1)
        ck_i16 = jnp.pad(flat(ck_i), ((0, 0), (0, 0), (0, LANES - di)))
        kiw16_k = jnp.concatenate([ck_i16, kiw16], axis=1)
        kb16 = jnp.concatenate([flat(ck_b), kb16], axis=1)
        vb16 = jnp.concatenate([flat(cv_b), vb16], axis=1)
        n_keys = P + S
    else:
        kiw16_k = kiw16
        n_keys = S
    lp = _round_up(n_keys, KEY_TILE)
    if lp != n_keys:
        ka16, va16, kiw16_k, kb16, vb16 = (_pad_rows(t, lp) for t in (ka16, va16, kiw16_k, kb16, vb16))

    oa = _dsa(qa16, qi16, kiw, ka16, va16, kiw16_k, n_keys, pos_off, kv, hd, di)
    ob = _diff(qb16, kb16, vb16, lw["lams"], lw["subln_g"], n_keys, pos_off, hd, _lambda_init(li))
    wa = oa.shape[2]
    x = _mm_res([oa.reshape(B * S, wa), ob.reshape(B * S, -1)], [lw["w_out"][:wa], lw["w_out"][wa:]],
                x, gt2, 1.0, "out_proj")

    h = _norm_mod(x, lw["norm_ffn2"], sc3, sh3)
    a = _mm_up(h, lw["w1_2"], lw["w3_2"])
    x = _mm_res([a], [lw["w2_2"]], x, gt3, 0.5, "ffn_down")
    return x, (k_a, v_a, k_i, k_b, v_b)


def kernel(x_prompt, x_sample, cache_k_a, cache_v_a, cache_k_idx, cache_k_b, cache_v_b, c_prompt, c_sample, w_ada, b_ada, norm_ffn1, w1_ffn1, w3_ffn1, w2_ffn1, norm_attn, w_in, qnorm_a, knorm_a, knorm_idx, qnorm_b, knorm_b, lam_q1, lam_k1, lam_q2, lam_k2, subln_b, w_out, norm_ffn2, w1_ffn2, w3_ffn2, w2_ffn2):
    depth = w_ada.shape[0]
    Bp, Sp, D = x_prompt.shape
    Bs, Ss, _ = x_sample.shape
    P = cache_k_a.shape[2]
    kv, hd = cache_k_a.shape[3], cache_k_a.shape[4]
    di = cache_k_idx.shape[3]
    hb = cache_k_b.shape[3]
    db = cache_v_b.shape[4]
    ha = (w_out.shape[1] - hb * db) // hd
    lay = _Layout(ha, kv, hb, hd, di)
    tables = _rope_tables(hd, di)

    xp = x_prompt.reshape(Bp * Sp, D)
    xs = x_sample.reshape(Bs * Ss, D)
    n_c = Bp + Bs
    c_all = jnp.pad(jnp.concatenate([c_prompt, c_sample], axis=0), ((0, _round_up(n_c, 8) - n_c), (0, 0)))

    rows_p, rows_s = [], []
    for l in range(depth):
        row = lambda a: a[l][None, :]
        gki = jnp.pad(knorm_idx[l], (0, LANES - di))[None, :]
        lw = dict(
            lay=lay, tables=tables,
            norm_ffn1=row(norm_ffn1), norm_attn=row(norm_attn), norm_ffn2=row(norm_ffn2),
            w1_1=w1_ffn1[l], w3_1=w3_ffn1[l], w2_1=w2_ffn1[l].astype(BF16),
            w1_2=w1_ffn2[l], w3_2=w3_ffn2[l], w2_2=w2_ffn2[l].astype(BF16),
            w_in=_prep_w_in(w_in[l], lay, N_IDX_HEADS), w_out=w_out[l].astype(BF16),
            gains=(row(qnorm_a), row(knorm_a), gki, row(qnorm_b), row(knorm_b)),
            lams=(row(lam_q1), row(lam_k1), row(lam_q2), row(lam_k2)),
            subln_g=row(subln_b),
        )
        mod = _ada(c_all, w_ada[l], b_ada[l][None, :])

        def mods_for(lo, nb, s):
            m = mod[lo:lo + nb].reshape(nb, N_MOD, D)
            if nb == 1:
                return tuple(m[:, i] for i in range(N_MOD))
            return tuple(jnp.repeat(m[:, i], s, axis=0) for i in range(N_MOD))

        xp, rp = _layer(xp, Bp, Sp, mods_for(0, Bp, Sp), None, 0, lw, l)
        past = (cache_k_a[l], cache_v_a[l], cache_k_idx[l], cache_k_b[l], cache_v_b[l])
        xs, rs = _layer(xs, Bs, Ss, mods_for(Bp, Bs, Ss), past, P, lw, l)
        rows_p.append(rp)
        rows_s.append(rs)

    def stack(rows, idx, B, S, tail):
        return jnp.stack([r[idx].reshape((B, S) + tail) for r in rows])

    tails = ((kv, hd), (kv, hd), (di,), (hb, 2, hd), (hb, db))
    outs_p = tuple(stack(rows_p, i, Bp, Sp, tails[i]) for i in range(5))
    outs_s = tuple(stack(rows_s, i, Bs, Ss, tails[i]) for i in range(5))
    return (xp.reshape(Bp, Sp, D), xs.reshape(Bs, Ss, D)) + outs_p + outs_s
```

```python
import functools
import math

import numpy as np
import jax
import jax.numpy as jnp
from jax import lax
from jax.experimental import pallas as pl
from jax.experimental.pallas import tpu as pltpu

F32 = jnp.float32
BF16 = jnp.bfloat16

CHUNK = 64
N_IDX_HEADS = 16
IDX_STACK = 4
TOPK_MAX = 256
ROPE_THETA = 500000.0
ROT_FRAC = 4
EPS = 1e-6
N_MOD = 9

LANES = 128
V7X_VMEM_BYTES = 64 * 1024 * 1024
VMEM_LIMIT = V7X_VMEM_BYTES * 7 // 8
NEG = -1e30
NT_DIMS = (((1,), (1,)), ((), ()))


def _lambda_init(layer_idx):
    return 0.8 - 0.6 * math.exp(-0.3 * layer_idx)


def _pick(n, candidates):
    for c in candidates:
        if n % c == 0:
            return c
    return n


def _cparams(semantics):
    return pltpu.CompilerParams(dimension_semantics=semantics, vmem_limit_bytes=VMEM_LIMIT)


def _round_up(n, m):
    return (n + m - 1) // m * m


def _ada_kernel(c_ref, w_ref, b_ref, o_ref):
    c = c_ref[...]
    s = (c * jax.nn.sigmoid(c)).astype(BF16)
    o_ref[...] = jnp.dot(s, w_ref[...].astype(BF16), preferred_element_type=F32) + b_ref[...]


def _ada(c, w, b):
    R, D = c.shape
    N = w.shape[1]
    tn = _pick(N, (512, 256, 128))
    return pl.pallas_call(
        _ada_kernel,
        grid=(N // tn,),
        in_specs=[pl.BlockSpec((R, D), lambda j: (0, 0)),
                  pl.BlockSpec((D, tn), lambda j: (0, j)),
                  pl.BlockSpec((1, tn), lambda j: (0, j))],
        out_specs=pl.BlockSpec((R, tn), lambda j: (0, j)),
        out_shape=jax.ShapeDtypeStruct((R, N), F32),
        compiler_params=_cparams(("parallel",)),
        name="ada_matvec",
    )(c, w, b)


def _norm_mod_kernel(x_ref, g_ref, sc_ref, sh_ref, o_ref):
    x = x_ref[...]
    ms = jnp.mean(x * x, axis=-1, keepdims=True)
    y = x * lax.rsqrt(ms + EPS) * g_ref[...]
    o_ref[...] = (y * (1.0 + sc_ref[...]) + sh_ref[...]).astype(o_ref.dtype)


def _norm_mod(x, g, scale, shift):
    M, D = x.shape
    tm = _pick(M, (512, 256, 128, 64, 32, 16, 8))

    def mspec(a):
        if a.shape[0] == 1:
            return pl.BlockSpec((1, D), lambda i: (0, 0))
        return pl.BlockSpec((tm, D), lambda i: (i, 0))

    return pl.pallas_call(
        _norm_mod_kernel,
        grid=(M // tm,),
        in_specs=[pl.BlockSpec((tm, D), lambda i: (i, 0)),
                  pl.BlockSpec((1, D), lambda i: (0, 0)),
                  mspec(scale), mspec(shift)],
        out_specs=pl.BlockSpec((tm, D), lambda i: (i, 0)),
        out_shape=jax.ShapeDtypeStruct((M, D), BF16),
        compiler_params=_cparams(("parallel",)),
        name="norm_mod",
    )(x, g, scale, shift)


def _mm_up_kernel(h_ref, w1_ref, w3_ref, o_ref):
    h = h_ref[...]
    u = jnp.dot(h, w1_ref[...].astype(BF16), preferred_element_type=F32)
    v = jnp.dot(h, w3_ref[...].astype(BF16), preferred_element_type=F32)
    o_ref[...] = (u * jax.nn.sigmoid(u) * v).astype(o_ref.dtype)


def _mm_up(h, w1, w3):
    M, K = h.shape
    F = w1.shape[1]
    tm = _pick(M, (2048, 1024, 512, 256, 128, 64, 32, 16, 8))
    tn = _pick(F, (256, 128))
    return pl.pallas_call(
        _mm_up_kernel,
        grid=(M // tm, F // tn),
        in_specs=[pl.BlockSpec((tm, K), lambda i, j: (i, 0), pipeline_mode=pl.Buffered(1)),
                  pl.BlockSpec((K, tn), lambda i, j: (0, j)),
                  pl.BlockSpec((K, tn), lambda i, j: (0, j))],
        out_specs=pl.BlockSpec((tm, tn), lambda i, j: (i, j)),
        out_shape=jax.ShapeDtypeStruct((M, F), BF16),
        compiler_params=_cparams(("parallel", "parallel")),
        name="ffn_up",
    )(h, w1, w3)


def _mm_plain_kernel(h_ref, w_ref, o_ref):
    o_ref[...] = jnp.dot(h_ref[...], w_ref[...], preferred_element_type=F32)


def _mm_plain(h, w):
    M, K = h.shape
    N = w.shape[1]
    tm = _pick(M, (1024, 512, 256, 128, 64, 32, 16, 8))
    tn = _pick(N, (512, 256, 128))
    return pl.pallas_call(
        _mm_plain_kernel,
        grid=(M // tm, N // tn),
        in_specs=[pl.BlockSpec((tm, K), lambda i, j: (i, 0)),
                  pl.BlockSpec((K, tn), lambda i, j: (0, j))],
        out_specs=pl.BlockSpec((tm, tn), lambda i, j: (i, j)),
        out_shape=jax.ShapeDtypeStruct((M, N), F32),
        compiler_params=_cparams(("parallel", "parallel")),
        name="in_proj",
    )(h, w)


def _mm_res_kernel(*refs, n_lhs, coef):
    a_refs = refs[:n_lhs]
    w_refs = refs[n_lhs:2 * n_lhs]
    x_ref, g_ref, o_ref = refs[2 * n_lhs:]
    y = jnp.dot(a_refs[0][...], w_refs[0][...], preferred_element_type=F32)
    for a_ref, w_ref in zip(a_refs[1:], w_refs[1:]):
        y = y + jnp.dot(a_ref[...], w_ref[...], preferred_element_type=F32)
    o_ref[...] = x_ref[...] + (coef * g_ref[...]) * y


def _mm_res(a_list, w_list, x, gate, coef, name):
    M, N = x.shape
    ktot = sum(a.shape[1] for a in a_list)
    tm = _pick(M, (1024, 512, 256, 128, 64, 32, 16, 8))
    tn = _pick(N, (512, 256, 128)) if ktot <= 4096 else _pick(N, (256, 128))
    lhs_mode = {} if ktot <= 4096 else dict(pipeline_mode=pl.Buffered(1))
    in_specs = [pl.BlockSpec((tm, a.shape[1]), lambda i, j: (i, 0), **lhs_mode) for a in a_list]
    in_specs += [pl.BlockSpec((w.shape[0], tn), lambda i, j: (0, j)) for w in w_list]
    in_specs.append(pl.BlockSpec((tm, tn), lambda i, j: (i, j)))
    if gate.shape[0] == 1:
        in_specs.append(pl.BlockSpec((1, tn), lambda i, j: (0, j)))
    else:
        in_specs.append(pl.BlockSpec((tm, tn), lambda i, j: (i, j)))
    return pl.pallas_call(
        functools.partial(_mm_res_kernel, n_lhs=len(a_list), coef=coef),
        grid=(M // tm, N // tn),
        in_specs=in_specs,
        out_specs=pl.BlockSpec((tm, tn), lambda i, j: (i, j)),
        out_shape=jax.ShapeDtypeStruct((M, N), F32),
        compiler_params=_cparams(("parallel", "parallel")),
        name=name,
    )(*a_list, *w_list, x, gate)


class _Layout:
    def __init__(self, ha, kv, hb, hd, di):
        self.ha, self.kv, self.hb, self.hd, self.di = ha, kv, hb, hd, di
        self.q_a = 0
        self.k_a = self.q_a + ha * hd
        self.v_a = self.k_a + kv * hd
        self.q_i = self.v_a + kv * hd
        self.kiw = self.q_i + N_IDX_HEADS * di
        self.q_b = self.kiw + LANES
        self.k_b = self.q_b + hb * 2 * hd
        self.v_b = self.k_b + hb * 2 * hd
        self.end = self.v_b + hb * 2 * hd
        self.width = _round_up(self.end, 512)


def _rope_tables(hd, di):
    def table(d):
        r = d // ROT_FRAC
        half = r // 2
        inv = ROPE_THETA ** (-(jnp.arange(half, dtype=jnp.float32) * 2.0 / r))
        lane = np.arange(LANES) % d
        idx = np.where(lane < half, lane, np.where(lane < r, lane - half, 0))
        invl = jnp.where(lane < r, inv[idx], 0.0)
        sign = np.where(lane < half, -1.0, np.where(lane < r, 1.0, 0.0)).astype(np.float32)
        src = np.where(lane < half, np.arange(LANES) + half, np.arange(LANES) - half)
        partner = np.zeros((LANES, LANES), np.float32)
        partner[src[lane < r], np.arange(LANES)[lane < r]] = 1.0
        return jnp.stack([invl, jnp.asarray(sign)])[:, None, :], jnp.asarray(partner, dtype=BF16)
    th, ph = table(hd)
    ti, pi = table(di)
    return th, ph, ti, pi


def _split3(x):
    x1 = x.astype(BF16)
    r1 = x - x1.astype(F32)
    x2 = r1.astype(BF16)
    x3 = (r1 - x2.astype(F32)).astype(BF16)
    return x1, x2, x3


def _post_kernel(z_ref, gqa_ref, gka_ref, gki_ref, gqb_ref, gkb_ref, th_ref, ph_ref, ti_ref, pi_ref,
                 ka_ref, va_ref, ki_ref, kb_ref, vb_ref,
                 qa16_ref, ka16_ref, va16_ref, qi16_ref, kiw_ref, kiw16_ref, qb16_ref, kb16_ref, vb16_ref,
                 *, lay, ts, pos_off, scale):
    hd, di = lay.hd, lay.di
    row = lax.broadcasted_iota(jnp.int32, (ts, LANES), 0)
    pos = (pos_off + pl.program_id(1) * ts + row).astype(F32)

    def rope_fn(t_ref, p_ref):
        ang = pos * t_ref[0]
        cosv = jnp.cos(ang)
        sinv = jnp.sin(ang) * t_ref[1]

        def rope(x):
            partner = sum(jnp.dot(t, p_ref[...], preferred_element_type=F32) for t in _split3(x))
            return x * cosv + partner * sinv
        return rope

    rope_h = rope_fn(th_ref, ph_ref)
    rope_i = rope_fn(ti_ref, pi_ref)

    def hnorm(x, g):
        ms = jnp.mean(x * x, axis=-1, keepdims=True)
        return x * lax.rsqrt(ms + EPS) * g

    def slab(off, h):
        return z_ref[0, :, off + h * hd: off + (h + 1) * hd]

    def cols(h):
        return slice(h * hd, (h + 1) * hd)

    for h in range(lay.ha):
        q = rope_h(hnorm(slab(lay.q_a, h), gqa_ref[...]))
        qa16_ref[0, :, cols(h)] = (q * scale).astype(BF16)
    for h in range(lay.kv):
        k = rope_h(hnorm(slab(lay.k_a, h), gka_ref[...]))
        ka_ref[0, :, cols(h)] = k
        ka16_ref[0, :, cols(h)] = k.astype(BF16)
        v = slab(lay.v_a, h)
        va_ref[0, :, cols(h)] = v
        va16_ref[0, :, cols(h)] = v.astype(BF16)
    for h in range(N_IDX_HEADS * di // LANES):
        qi16_ref[0, :, cols(h)] = rope_i(slab(lay.q_i, h)).astype(BF16)
    x = slab(lay.kiw, 0)
    lane = lax.broadcasted_iota(jnp.int32, (ts, LANES), 1)
    is_k = lane < di
    ms = jnp.sum(jnp.where(is_k, x * x, 0.0), axis=-1, keepdims=True) * (1.0 / di)
    ki = rope_i(x * lax.rsqrt(ms + EPS) * gki_ref[...])
    ki_ref[0] = ki[:, :di]
    kiw = jnp.where(is_k, ki, x)
    kiw_ref[0] = kiw
    kiw16_ref[0] = kiw.astype(BF16)
    for h in range(2 * lay.hb):
        q = rope_h(hnorm(slab(lay.q_b, h), gqb_ref[...]))
        qb16_ref[0, :, cols(h)] = (q * scale).astype(BF16)
        k = rope_h(hnorm(slab(lay.k_b, h), gkb_ref[...]))
        kb_ref[0, :, cols(h)] = k
        kb16_ref[0, :, cols(h)] = k.astype(BF16)
        v = slab(lay.v_b, h)
        vb_ref[0, :, cols(h)] = v
        vb16_ref[0, :, cols(h)] = v.astype(BF16)


def _post(z, lay, gains, tables, pos_off):
    B, S, W = z.shape
    hd, di = lay.hd, lay.di
    ts = _pick(S, (128, 64, 32, 16, 8))
    wa, wkv, wqi, wb = lay.ha * hd, lay.kv * hd, N_IDX_HEADS * di, lay.hb * 2 * hd

    def rows(width):
        return pl.BlockSpec((1, ts, width), lambda b, s: (b, s, 0))

    def const(a):
        return pl.BlockSpec(a.shape, lambda b, s: (0,) * a.ndim)

    out_widths = [(wkv, F32), (wkv, F32), (di, F32), (wb, F32), (wb, F32),
                  (wa, BF16), (wkv, BF16), (wkv, BF16), (wqi, BF16), (LANES, F32), (LANES, BF16),
                  (wb, BF16), (wb, BF16), (wb, BF16)]
    return pl.pallas_call(
        functools.partial(_post_kernel, lay=lay, ts=ts, pos_off=pos_off,
                          scale=hd ** -0.5 * math.log2(math.e)),
        grid=(B, S // ts),
        in_specs=[rows(W)] + [const(g) for g in gains] + [const(t) for t in tables],
        out_specs=[rows(w) for w, _ in out_widths],
        out_shape=[jax.ShapeDtypeStruct((B, S, w), dt) for w, dt in out_widths],
        compiler_params=_cparams(("parallel", "parallel")),
        name="head_norm_rope",
    )(z, *gains, *tables)


def _tile_bounds(p0, tq, tk, n_keys):
    a_full = jnp.minimum((p0 // CHUNK + 1) * CHUNK, n_keys)
    a_tot = jnp.minimum(((p0 + tq - 1) // CHUNK + 1) * CHUNK, n_keys)
    return a_full // tk, (a_tot + tk - 1) // tk


def _allowed(j, p0, tq, tk, n_keys):
    shift = CHUNK.bit_length() - 1
    kpos = j * tk + lax.broadcasted_iota(jnp.int32, (tq, tk), 1)
    qpos = p0 + lax.broadcasted_iota(jnp.int32, (tq, tk), 0)
    return (lax.shift_right_logical(kpos, shift) <= lax.shift_right_logical(qpos, shift)) & (kpos < n_keys)


def _lane_tile_list(parts):
    return parts[0] if len(parts) == 1 else jnp.concatenate(parts, axis=1)


def _lane_tile(x, reps):
    return _lane_tile_list([x] * reps)


KEY_TILE = 256
KEY_TILE_WIDE = 1024


def _key_tile(lp):
    return KEY_TILE_WIDE if lp % KEY_TILE_WIDE == 0 else KEY_TILE


def _lane_tile_max(s):
    out = s[:, :LANES]
    for c in range(1, s.shape[1] // LANES):
        out = jnp.maximum(out, s[:, c * LANES:(c + 1) * LANES])
    return out


def _softmax_step(s, v, m_ref, l_ref, acc_ref, idx, tile_max=None):
    tk = s.shape[1]
    wv = acc_ref.shape[-1]
    m_old = m_ref[idx]
    if tile_max is None:
        tile_max = _lane_tile_max(s)
    m_new = jnp.maximum(m_old, jnp.max(tile_max, axis=1, keepdims=True))
    alpha = jnp.exp2(m_old - m_new)
    p = jnp.exp2(s - _lane_tile(m_new, tk // LANES))
    if l_ref is not None:
        l_ref[idx] = alpha * l_ref[idx] + jnp.sum(p, axis=1, keepdims=True)
    acc_ref[idx] = _lane_tile(alpha, wv // LANES) * acc_ref[idx] + jnp.dot(
        p.astype(BF16), v, preferred_element_type=F32)
    m_ref[idx] = m_new


def _dsa_kernel(qa_ref, qi_ref, wq_ref, ka_ref, va_ref, ki_ref, rank_ref, o_ref,
                keys_ref, wexp_ref, qih_ref, tau_ref, quota_ref, seen_ref, bias_ref, m_ref, acc_ref, s_ref, mx_ref,
                *, tq, tk, n_keys, pos_off, topk, ha, kv, hd, di):
    p0 = pos_off + pl.program_id(1) * tq
    n_full, n_tot = _tile_bounds(p0, tq, tk, n_keys)
    reps = tk // LANES
    group = ha // kv
    int_min = jnp.int32(-2 ** 31)

    def rows(r):
        return slice(r * tq, (r + 1) * tq)

    lane = lax.broadcasted_iota(jnp.int32, (tq, LANES), 1)
    w_scale = (N_IDX_HEADS ** -0.5) * (di ** -0.5)
    for h in range(N_IDX_HEADS):
        pair = qi_ref[0, :, (h * di // LANES) * LANES:(h * di // LANES + 1) * LANES].astype(F32)
        off = (h * di) % LANES
        if off:
            pair = pltpu.roll(pair, LANES - off, 1)
        qih_ref[h // IDX_STACK, rows(h % IDX_STACK), :] = jnp.where(lane < di, pair, 0.0).astype(BF16)
        wexp_ref[h] = jnp.broadcast_to(wq_ref[0, :, di + h:di + h + 1] * w_scale, (tq, LANES))

    def key_slice(j):
        return pl.ds(pl.multiple_of(j * tk, tk), tk)

    def score_tile(j, masked):
        kt = ki_ref[0, key_slice(j), :]
        parts = [jnp.zeros((tq, LANES), F32)] * reps
        for hs in range(N_IDX_HEADS // IDX_STACK):
            lg = lax.dot_general(qih_ref[hs], kt, NT_DIMS, preferred_element_type=F32)
            for r in range(IDX_STACK):
                w = wexp_ref[hs * IDX_STACK + r]
                parts = [parts[c] + jnp.maximum(lg[rows(r), c * LANES:(c + 1) * LANES], 0.0) * w
                         for c in range(reps)]
        sc = _lane_tile_list(parts)
        if masked:
            sc = jnp.where(_allowed(j, p0, tq, tk, n_keys), sc, -jnp.inf)
        bits = pltpu.bitcast(sc, jnp.int32)
        keys_ref[j] = jnp.where(bits < 0, bits ^ jnp.int32(0x7FFFFFFF), bits)

    def score_full(j, c):
        score_tile(j, False)
        return c

    def score_masked(j, c):
        score_tile(j, True)
        return c

    lax.fori_loop(0, n_full, score_full, 0)
    lax.fori_loop(n_full, n_tot, score_masked, 0)

    def count_ge(cand):
        def body(j, acc):
            t = keys_ref[j]
            for c in range(reps):
                acc = jnp.where(t[:, c * LANES:(c + 1) * LANES] >= cand, acc + 1.0, acc)
            return acc
        acc = lax.fori_loop(0, n_tot, body, jnp.zeros((tq, LANES), F32))
        return jnp.sum(acc, axis=1, keepdims=True)

    kf = float(topk)
    zero = jnp.zeros((tq, LANES), jnp.int32)
    c0 = jnp.broadcast_to(count_ge(zero), (tq, LANES))
    n_all = jnp.broadcast_to((n_tot * tk).astype(F32), (tq, LANES))
    state = (jnp.int32(0), jnp.where(c0 >= kf, zero, int_min), jnp.where(c0 >= kf, c0, n_all))

    def bit_cond(state):
        b, _, cnt = state
        return (b < 31) & (jnp.max(jnp.where(cnt == kf, 0.0, 1.0)) > 0.0)

    def bit_body(state):
        b, prefix, cnt = state
        cand = prefix + lax.shift_left(jnp.int32(1), 30 - b)
        c = jnp.broadcast_to(count_ge(cand), (tq, LANES))
        return b + 1, jnp.where(c >= kf, cand, prefix), jnp.where(c >= kf, c, cnt)

    _, tau, n_ge = lax.while_loop(bit_cond, bit_body, state)
    tau_ref[...] = tau

    m_ref[...] = jnp.full(m_ref.shape, NEG, F32)
    acc_ref[...] = jnp.zeros(acc_ref.shape, F32)
    ones = jnp.ones((tk, hd), BF16)

    def selection_bias(j, tied):
        key = keys_ref[j]
        if not tied:
            return jnp.where(key >= _lane_tile(tau_ref[...], reps), 0.0, NEG)
        tau_t = tau_ref[...]
        quota = quota_ref[...]
        seen = seen_ref[...]
        parts = []
        for c in range(reps):
            kc = key[:, c * LANES:(c + 1) * LANES]
            tie = kc == tau_t
            cnt = jnp.dot(jnp.where(tie, 1.0, 0.0).astype(BF16), rank_ref[...], preferred_element_type=F32)
            sel = (kc > tau_t) | (tie & (seen + cnt[:, :LANES] < quota))
            parts.append(jnp.where(sel, 0.0, NEG))
            seen = seen + cnt[:, LANES:]
        seen_ref[...] = seen
        return _lane_tile_list(parts)

    def attn_tile(j, masked, tied):
        bias = selection_bias(j, tied)
        if masked:
            bias = jnp.where(_allowed(j, p0, tq, tk, n_keys), bias, NEG)
        bias_ref[...] = bias
        def produce(h):
            k = ka_ref[0, key_slice(j), (h // group) * hd:(h // group + 1) * hd]
            s = lax.dot_general(qa_ref[0, :, h * hd:(h + 1) * hd], k, NT_DIMS, preferred_element_type=F32)
            s = s + bias_ref[...]
            s_ref[h % 2] = s
            mx_ref[h % 2] = _lane_tile_max(s)

        def consume(h):
            g = h // group
            v1 = jnp.concatenate([va_ref[0, key_slice(j), g * hd:(g + 1) * hd], ones], axis=1)
            _softmax_step(s_ref[h % 2], v1, m_ref, None, acc_ref, h, tile_max=mx_ref[h % 2])

        produce(0)
        for h in range(ha):
            if h + 1 < ha:
                produce(h + 1)
            consume(h)

    def attend(tied):
        def attn_full(j, c):
            attn_tile(j, False, tied)
            return c

        def attn_masked(j, c):
            attn_tile(j, True, tied)
            return c

        lax.fori_loop(0, n_full, attn_full, 0)
        lax.fori_loop(n_full, n_tot, attn_masked, 0)

    any_tied = jnp.max(jnp.where(n_ge > kf, 1.0, 0.0)) > 0.0

    @pl.when(any_tied)
    def _():
        quota_ref[...] = kf - jnp.broadcast_to(count_ge(tau_ref[...] + 1), (tq, LANES))
        seen_ref[...] = jnp.zeros((tq, LANES), F32)
        attend(True)

    @pl.when(jnp.logical_not(any_tied))
    def _():
        attend(False)
    for h in range(ha):
        o_ref[0, :, h * hd:(h + 1) * hd] = (acc_ref[h, :, :hd] / acc_ref[h, :, hd:]).astype(o_ref.dtype)


def _dsa(qa, qi, kiw_q, ka, va, kiw_k, n_keys, pos_off, kv, hd, di):
    B, S, wa = qa.shape
    lp = ka.shape[1]
    ha = wa // hd
    tq = _pick(S, (128, 64, 32, 16, 8))
    tk = _key_tile(lp)
    topk = min(TOPK_MAX, n_keys // 4)
    kern = functools.partial(_dsa_kernel, tq=tq, tk=tk, n_keys=n_keys, pos_off=pos_off, topk=topk,
                             ha=ha, kv=kv, hd=hd, di=di)

    def rows(width):
        return pl.BlockSpec((1, tq, width), lambda b, i: (b, i, 0))

    def resident(width):
        return pl.BlockSpec((1, lp, width), lambda b, i: (b, 0, 0), pipeline_mode=pl.Buffered(1))

    lane = np.arange(LANES)
    rank_mat = jnp.asarray(np.concatenate([(lane[:, None] < lane[None, :]).astype(np.float32),
                                           np.ones((LANES, LANES), np.float32)], axis=1), dtype=BF16)
    return pl.pallas_call(
        kern,
        grid=(B, S // tq),
        in_specs=[rows(wa), rows(qi.shape[2]), rows(LANES),
                  resident(kv * hd), resident(kv * hd), resident(LANES),
                  pl.BlockSpec((LANES, 2 * LANES), lambda b, i: (0, 0))],
        out_specs=rows(wa),
        out_shape=jax.ShapeDtypeStruct((B, S, wa), BF16),
        scratch_shapes=[pltpu.VMEM((lp // tk, tq, tk), jnp.int32),
                        pltpu.VMEM((N_IDX_HEADS, tq, LANES), F32),
                        pltpu.VMEM((N_IDX_HEADS // IDX_STACK, IDX_STACK * tq, LANES), BF16),
                        pltpu.VMEM((tq, LANES), jnp.int32),
                        pltpu.VMEM((tq, LANES), F32),
                        pltpu.VMEM((tq, LANES), F32),
                        pltpu.VMEM((tq, tk), F32),
                        pltpu.VMEM((ha, tq, LANES), F32),
                        pltpu.VMEM((ha, tq, 2 * hd), F32),
                        pltpu.VMEM((2, tq, tk), F32),
                        pltpu.VMEM((2, tq, LANES), F32)],
        compiler_params=_cparams(("parallel", "arbitrary")),
        name="dsa_attention",
    )(qa, qi, kiw_q, ka, va, kiw_k, rank_mat)


def _diff_kernel(q_ref, k_ref, v_ref, lq1_ref, lk1_ref, lq2_ref, lk2_ref, sg_ref, o_ref,
                 m_ref, l_ref, acc_ref, s_ref, mx_ref, *, tq, tk, n_keys, pos_off, hd, lam_init):
    p0 = pos_off + pl.program_id(2) * tq
    n_full, n_tot = _tile_bounds(p0, tq, tk, n_keys)
    db = 2 * hd
    m_ref[...] = jnp.full(m_ref.shape, NEG, F32)
    l_ref[...] = jnp.zeros(l_ref.shape, F32)
    acc_ref[...] = jnp.zeros(acc_ref.shape, F32)

    def key_slice(j):
        return pl.ds(pl.multiple_of(j * tk, tk), tk)

    def scores(j, c):
        return lax.dot_general(q_ref[0, :, c * hd:(c + 1) * hd], k_ref[0, key_slice(j), c * hd:(c + 1) * hd],
                               NT_DIMS, preferred_element_type=F32)

    def produce(j, slot, masked=False):
        ok = _allowed(j, p0, tq, tk, n_keys) if masked else None
        for c in range(2):
            s = scores(j, c)
            if masked:
                s = jnp.where(ok, s, NEG)
            s_ref[slot, c] = s
            mx_ref[slot, c] = _lane_tile_max(s)

    def consume(j, slot):
        v = v_ref[0, key_slice(j), :]
        for c in range(2):
            _softmax_step(s_ref[slot, c], v, m_ref, l_ref, acc_ref, c, tile_max=mx_ref[slot, c])

    n_pipe = jnp.minimum(n_full + 1, n_tot)
    n_pairs = jnp.maximum(n_pipe - 2, 0) // 2
    first = 2 * n_pairs
    left = n_pipe - first

    @pl.when(n_pipe >= 2)
    def _():
        produce(0, 0)

    def pair_body(t, carry):
        produce(2 * t + 1, 1)
        consume(2 * t, 0)
        produce(2 * t + 2, 0)
        consume(2 * t + 1, 1)
        return carry

    lax.fori_loop(0, n_pairs, pair_body, 0)

    @pl.when(left == 1)
    def _():
        produce(first, 0, masked=True)
        consume(first, 0)

    @pl.when(left == 2)
    def _():
        produce(first + 1, 1, masked=True)
        consume(first, 0)
        consume(first + 1, 1)

    @pl.when(left == 3)
    def _():
        produce(first + 1, 1)
        consume(first, 0)
        produce(first + 2, 0, masked=True)
        consume(first + 1, 1)
        consume(first + 2, 0)

    def tile_masked(j, carry):
        produce(j, 0, masked=True)
        consume(j, 0)
        return carry

    lax.fori_loop(n_pipe, n_tot, tile_masked, 0)

    lam = (jnp.exp(jnp.sum(lq1_ref[...] * lk1_ref[...], axis=-1, keepdims=True))
           - jnp.exp(jnp.sum(lq2_ref[...] * lk2_ref[...], axis=-1, keepdims=True)) + lam_init)
    reps = db // LANES
    o = acc_ref[0] / _lane_tile(l_ref[0], reps) - lam * (acc_ref[1] / _lane_tile(l_ref[1], reps))
    ms = jnp.mean(o * o, axis=-1, keepdims=True)
    o_ref[0] = (o * lax.rsqrt(ms + EPS) * sg_ref[...] * (1.0 - lam_init)).astype(o_ref.dtype)


def _diff(qb, kb, vb, lams, subln_g, n_keys, pos_off, hd, lam_init):
    B, S, wb = qb.shape
    lp = kb.shape[1]
    db = 2 * hd
    hb = wb // db
    tq = _pick(S, (512, 256, 128, 64, 32, 16, 8))
    tk = _key_tile(lp)
    kern = functools.partial(_diff_kernel, tq=tq, tk=tk, n_keys=n_keys, pos_off=pos_off, hd=hd,
                             lam_init=lam_init)
    vec = pl.BlockSpec((1, hd), lambda b, h, i: (0, 0))
    return pl.pallas_call(
        kern,
        grid=(B, hb, S // tq),
        in_specs=[pl.BlockSpec((1, tq, db), lambda b, h, i: (b, i, h)),
                  pl.BlockSpec((1, lp, db), lambda b, h, i: (b, 0, h)),
                  pl.BlockSpec((1, lp, db), lambda b, h, i: (b, 0, h)),
                  vec, vec, vec, vec,
                  pl.BlockSpec((1, db), lambda b, h, i: (0, 0))],
        out_specs=pl.BlockSpec((1, tq, db), lambda b, h, i: (b, i, h)),
        out_shape=jax.ShapeDtypeStruct((B, S, wb), BF16),
        scratch_shapes=[pltpu.VMEM((2, tq, LANES), F32),
                        pltpu.VMEM((2, tq, LANES), F32),
                        pltpu.VMEM((2, tq, db), F32),
                        pltpu.VMEM((2, 2, tq, tk), F32),
                        pltpu.VMEM((2, 2, tq, LANES), F32)],
        compiler_params=_cparams(("parallel", "parallel", "arbitrary")),
        name="diff_attention",
    )(qb, kb, vb, *lams, subln_g)


def _pad_rows(a, lp):
    return jnp.pad(a, ((0, 0), (0, lp - a.shape[1]), (0, 0)))


def _prep_w_in(w_in, lay, d_idx_w):
    ha, kv, hb, hd, di = lay.ha, lay.kv, lay.hb, lay.hd, lay.di
    widths = (ha * hd, kv * hd, kv * hd, N_IDX_HEADS * di, di, d_idx_w, hb * 2 * hd, hb * 2 * hd, hb * 2 * hd)
    offs = np.cumsum((0,) + widths)
    seg = [w_in[:, offs[i]:offs[i + 1]] for i in range(len(widths))]
    d = w_in.shape[0]
    pad_kiw = jnp.zeros((d, LANES - di - d_idx_w), w_in.dtype)
    pad_end = jnp.zeros((d, lay.width - lay.end), w_in.dtype)
    return jnp.concatenate(seg[:6] + [pad_kiw] + seg[6:] + [pad_end], axis=1).astype(BF16)


def _layer(x, B, S, mods, past, pos_off, lw, li):
    (sh1, sc1, gt1, sh2, sc2, gt2, sh3, sc3, gt3) = mods
    D = x.shape[1]
    lay = lw["lay"]
    hd, di, kv = lay.hd, lay.di, lay.kv

    h = _norm_mod(x, lw["norm_ffn1"], sc1, sh1)
    a = _mm_up(h, lw["w1_1"], lw["w3_1"])
    x = _mm_res([a], [lw["w2_1"]], x, gt1, 0.5, "ffn_down")

    h = _norm_mod(x, lw["norm_attn"], sc2, sh2)
    z = _mm_plain(h, lw["w_in"]).reshape(B, S, lay.width)
    (k_a, v_a, k_i, k_b, v_b, qa16, ka16, va16, qi16, kiw, kiw16, qb16, kb16, vb16) = _post(
        z, lay, lw["gains"], lw["tables"], pos_off)

    if past is not None:
        ck_a, cv_a, ck_i, ck_b, cv_b = past
        P = ck_a.shape[1]
        flat = lambda c: c.reshape(B, P, -1).astype(BF16)
        ka16 = jnp.concatenate([flat(ck_a), ka16], axis=1)
        va16 = jnp.concatenate([flat(cv_a), va16], axis=1)
        ck_i16 = jnp.pad(flat(ck_i), ((0, 0), (0, 0), (0, LANES - di)))
        kiw16_k = jnp.concatenate([ck_i16, kiw16], axis=1)
        kb16 = jnp.concatenate([flat(ck_b), kb16], axis=1)
        vb16 = jnp.concatenate([flat(cv_b), vb16], axis=1)
        n_keys = P + S
    else:
        kiw16_k = kiw16
        n_keys = S
    lp = _round_up(n_keys, KEY_TILE)
    if lp != n_keys:
        ka16, va16, kiw16_k, kb16, vb16 = (_pad_rows(t, lp) for t in (ka16, va16, kiw16_k, kb16, vb16))

    oa = _dsa(qa16, qi16, kiw, ka16, va16, kiw16_k, n_keys, pos_off, kv, hd, di)
    ob = _diff(qb16, kb16, vb16, lw["lams"], lw["subln_g"], n_keys, pos_off, hd, _lambda_init(li))
    wa = oa.shape[2]
    x = _mm_res([oa.reshape(B * S, wa), ob.reshape(B * S, -1)], [lw["w_out"][:wa], lw["w_out"][wa:]],
                x, gt2, 1.0, "out_proj")

    h = _norm_mod(x, lw["norm_ffn2"], sc3, sh3)
    a = _mm_up(h, lw["w1_2"], lw["w3_2"])
    x = _mm_res([a], [lw["w2_2"]], x, gt3, 0.5, "ffn_down")
    return x, (k_a, v_a, k_i, k_b, v_b)


def kernel(x_prompt, x_sample, cache_k_a, cache_v_a, cache_k_idx, cache_k_b, cache_v_b, c_prompt, c_sample, w_ada, b_ada, norm_ffn1, w1_ffn1, w3_ffn1, w2_ffn1, norm_attn, w_in, qnorm_a, knorm_a, knorm_idx, qnorm_b, knorm_b, lam_q1, lam_k1, lam_q2, lam_k2, subln_b, w_out, norm_ffn2, w1_ffn2, w3_ffn2, w2_ffn2):
    depth = w_ada.shape[0]
    Bp, Sp, D = x_prompt.shape
    Bs, Ss, _ = x_sample.shape
    P = cache_k_a.shape[2]
    kv, hd = cache_k_a.shape[3], cache_k_a.shape[4]
    di = cache_k_idx.shape[3]
    hb = cache_k_b.shape[3]
    db = cache_v_b.shape[4]
    ha = (w_out.shape[1] - hb * db) // hd
    lay = _Layout(ha, kv, hb, hd, di)
    tables = _rope_tables(hd, di)

    xp = x_prompt.reshape(Bp * Sp, D)
    xs = x_sample.reshape(Bs * Ss, D)
    n_c = Bp + Bs
    c_all = jnp.pad(jnp.concatenate([c_prompt, c_sample], axis=0), ((0, _round_up(n_c, 8) - n_c), (0, 0)))

    rows_p, rows_s = [], []
    for l in range(depth):
        row = lambda a: a[l][None, :]
        gki = jnp.pad(knorm_idx[l], (0, LANES - di))[None, :]
        lw = dict(
            lay=lay, tables=tables,
            norm_ffn1=row(norm_ffn1), norm_attn=row(norm_attn), norm_ffn2=row(norm_ffn2),
            w1_1=w1_ffn1[l], w3_1=w3_ffn1[l], w2_1=w2_ffn1[l].astype(BF16),
            w1_2=w1_ffn2[l], w3_2=w3_ffn2[l], w2_2=w2_ffn2[l].astype(BF16),
            w_in=_prep_w_in(w_in[l], lay, N_IDX_HEADS), w_out=w_out[l].astype(BF16),
            gains=(row(qnorm_a), row(knorm_a), gki, row(qnorm_b), row(knorm_b)),
            lams=(row(lam_q1), row(lam_k1), row(lam_q2), row(lam_k2)),
            subln_g=row(subln_b),
        )
        mod = _ada(c_all, w_ada[l], b_ada[l][None, :])

        def mods_for(lo, nb, s):
            m = mod[lo:lo + nb].reshape(nb, N_MOD, D)
            if nb == 1:
                return tuple(m[:, i] for i in range(N_MOD))
            return tuple(jnp.repeat(m[:, i], s, axis=0) for i in range(N_MOD))

        xp, rp = _layer(xp, Bp, Sp, mods_for(0, Bp, Sp), None, 0, lw, l)
        past = (cache_k_a[l], cache_v_a[l], cache_k_idx[l], cache_k_b[l], cache_v_b[l])
        xs, rs = _layer(xs, Bs, Ss, mods_for(Bp, Bs, Ss), past, P, lw, l)
        rows_p.append(rp)
        rows_s.append(rs)

    def stack(rows, idx, B, S, tail):
        return jnp.stack([r[idx].reshape((B, S) + tail) for r in rows])

    tails = ((kv, hd), (kv, hd), (di,), (hb, 2, hd), (hb, db))
    outs_p = tuple(stack(rows_p, i, Bp, Sp, tails[i]) for i in range(5))
    outs_s = tuple(stack(rows_s, i, Bs, Ss, tails[i]) for i in range(5))
    return (xp.reshape(Bp, Sp, D), xs.reshape(Bs, Ss, D)) + outs_p + outs_s
```

```python
import functools
import math

import numpy as np
import jax
import jax.numpy as jnp
from jax import lax
from jax.experimental import pallas as pl
from jax.experimental.pallas import tpu as pltpu

F32 = jnp.float32
BF16 = jnp.bfloat16

CHUNK = 64
N_IDX_HEADS = 16
IDX_STACK = 4
TOPK_MAX = 256
ROPE_THETA = 500000.0
ROT_FRAC = 4
EPS = 1e-6
N_MOD = 9

LANES = 128
V7X_VMEM_BYTES = 64 * 1024 * 1024
VMEM_LIMIT = V7X_VMEM_BYTES * 7 // 8
NEG = -1e30
NT_DIMS = (((1,), (1,)), ((), ()))


def _lambda_init(layer_idx):
    return 0.8 - 0.6 * math.exp(-0.3 * layer_idx)


def _pick(n, candidates):
    for c in candidates:
        if n % c == 0:
            return c
    return n


def _cparams(semantics):
    return pltpu.CompilerParams(dimension_semantics=semantics, vmem_limit_bytes=VMEM_LIMIT)


def _round_up(n, m):
    return (n + m - 1) // m * m


def _ada_kernel(c_ref, w_ref, b_ref, o_ref):
    c = c_ref[...]
    s = (c * jax.nn.sigmoid(c)).astype(BF16)
    o_ref[...] = jnp.dot(s, w_ref[...].astype(BF16), preferred_element_type=F32) + b_ref[...]


def _ada(c, w, b):
    R, D = c.shape
    N = w.shape[1]
    tn = _pick(N, (512, 256, 128))
    return pl.pallas_call(
        _ada_kernel,
        grid=(N // tn,),
        in_specs=[pl.BlockSpec((R, D), lambda j: (0, 0)),
                  pl.BlockSpec((D, tn), lambda j: (0, j)),
                  pl.BlockSpec((1, tn), lambda j: (0, j))],
        out_specs=pl.BlockSpec((R, tn), lambda j: (0, j)),
        out_shape=jax.ShapeDtypeStruct((R, N), F32),
        compiler_params=_cparams(("parallel",)),
        name="ada_matvec",
    )(c, w, b)


def _norm_mod_kernel(x_ref, g_ref, sc_ref, sh_ref, o_ref):
    x = x_ref[...]
    ms = jnp.mean(x * x, axis=-1, keepdims=True)
    y = x * lax.rsqrt(ms + EPS) * g_ref[...]
    o_ref[...] = (y * (1.0 + sc_ref[...]) + sh_ref[...]).astype(o_ref.dtype)


def _norm_mod(x, g, scale, shift):
    M, D = x.shape
    tm = _pick(M, (512, 256, 128, 64, 32, 16, 8))

    def mspec(a):
        if a.shape[0] == 1:
            return pl.BlockSpec((1, D), lambda i: (0, 0))
        return pl.BlockSpec((tm, D), lambda i: (i, 0))

    return pl.pallas_call(
        _norm_mod_kernel,
        grid=(M // tm,),
        in_specs=[pl.BlockSpec((tm, D), lambda i: (i, 0)),
                  pl.BlockSpec((1, D), lambda i: (0, 0)),
                  mspec(scale), mspec(shift)],
        out_specs=pl.BlockSpec((tm, D), lambda i: (i, 0)),
        out_shape=jax.ShapeDtypeStruct((M, D), BF16),
        compiler_params=_cparams(("parallel",)),
        name="norm_mod",
    )(x, g, scale, shift)


def _mm_up_kernel(h_ref, w1_ref, w3_ref, o_ref):
    h = h_ref[...]
    u = jnp.dot(h, w1_ref[...].astype(BF16), preferred_element_type=F32)
    v = jnp.dot(h, w3_ref[...].astype(BF16), preferred_element_type=F32)
    o_ref[...] = (u * jax.nn.sigmoid(u) * v).astype(o_ref.dtype)


def _mm_up(h, w1, w3):
    M, K = h.shape
    F = w1.shape[1]
    tm = _pick(M, (2048, 1024, 512, 256, 128, 64, 32, 16, 8))
    tn = _pick(F, (256, 128))
    return pl.pallas_call(
        _mm_up_kernel,
        grid=(M // tm, F // tn),
        in_specs=[pl.BlockSpec((tm, K), lambda i, j: (i, 0), pipeline_mode=pl.Buffered(1)),
                  pl.BlockSpec((K, tn), lambda i, j: (0, j)),
                  pl.BlockSpec((K, tn), lambda i, j: (0, j))],
        out_specs=pl.BlockSpec((tm, tn), lambda i, j: (i, j)),
        out_shape=jax.ShapeDtypeStruct((M, F), BF16),
        compiler_params=_cparams(("parallel", "parallel")),
        name="ffn_up",
    )(h, w1, w3)


def _mm_plain_kernel(h_ref, w_ref, o_ref):
    o_ref[...] = jnp.dot(h_ref[...], w_ref[...], preferred_element_type=F32)


def _mm_plain(h, w):
    M, K = h.shape
    N = w.shape[1]
    tm = _pick(M, (1024, 512, 256, 128, 64, 32, 16, 8))
    tn = _pick(N, (512, 256, 128))
    return pl.pallas_call(
        _mm_plain_kernel,
        grid=(M // tm, N // tn),
        in_specs=[pl.BlockSpec((tm, K), lambda i, j: (i, 0)),
                  pl.BlockSpec((K, tn), lambda i, j: (0, j))],
        out_specs=pl.BlockSpec((tm, tn), lambda i, j: (i, j)),
        out_shape=jax.ShapeDtypeStruct((M, N), F32),
        compiler_params=_cparams(("parallel", "parallel")),
        name="in_proj",
    )(h, w)


def _mm_res_kernel(*refs, n_lhs, coef):
    a_refs = refs[:n_lhs]
    w_refs = refs[n_lhs:2 * n_lhs]
    x_ref, g_ref, o_ref = refs[2 * n_lhs:]
    y = jnp.dot(a_refs[0][...], w_refs[0][...], preferred_element_type=F32)
    for a_ref, w_ref in zip(a_refs[1:], w_refs[1:]):
        y = y + jnp.dot(a_ref[...], w_ref[...], preferred_element_type=F32)
    o_ref[...] = x_ref[...] + (coef * g_ref[...]) * y


def _mm_res_ksplit_kernel(a_ref, w_ref, x_ref, g_ref, o_ref, acc_ref, *, coef, n_k):
    k = pl.program_id(2)
    part = jnp.dot(a_ref[...], w_ref[...], preferred_element_type=F32)

    @pl.when(k == 0)
    def _():
        acc_ref[...] = part

    @pl.when((k > 0) & (k < n_k - 1))
    def _():
        acc_ref[...] += part

    @pl.when(k == n_k - 1)
    def _():
        o_ref[...] = x_ref[...] + (coef * g_ref[...]) * (acc_ref[...] + part)


def _mm_res_ksplit(a, w, x, gate, coef, name, n_k):
    M, N = x.shape
    tk = a.shape[1] // n_k
    tm = _pick(M, (1024, 512, 256, 128, 64, 32, 16, 8))
    tn = _pick(N, (512, 256, 128))
    gspec = (pl.BlockSpec((1, tn), lambda i, j, k: (0, j)) if gate.shape[0] == 1
             else pl.BlockSpec((tm, tn), lambda i, j, k: (i, j)))
    return pl.pallas_call(
        functools.partial(_mm_res_ksplit_kernel, coef=coef, n_k=n_k),
        grid=(M // tm, N // tn, n_k),
        in_specs=[pl.BlockSpec((tm, tk), lambda i, j, k: (i, k)),
                  pl.BlockSpec((tk, tn), lambda i, j, k: (k, j)),
                  pl.BlockSpec((tm, tn), lambda i, j, k: (i, j)),
                  gspec],
        out_specs=pl.BlockSpec((tm, tn), lambda i, j, k: (i, j)),
        out_shape=jax.ShapeDtypeStruct((M, N), F32),
        scratch_shapes=[pltpu.VMEM((tm, tn), F32)],
        compiler_params=_cparams(("parallel", "parallel", "arbitrary")),
        name=name,
    )(a, w, x, gate)


def _mm_res(a_list, w_list, x, gate, coef, name):
    M, N = x.shape
    ktot = sum(a.shape[1] for a in a_list)
    if len(a_list) == 1 and ktot > 4096 and ktot % (2 * LANES) == 0:
        return _mm_res_ksplit(a_list[0], w_list[0], x, gate, coef, name, 2)
    tm = _pick(M, (1024, 512, 256, 128, 64, 32, 16, 8))
    tn = _pick(N, (512, 256, 128)) if ktot <= 4096 else _pick(N, (256, 128))
    lhs_mode = {} if ktot <= 4096 else dict(pipeline_mode=pl.Buffered(1))
    in_specs = [pl.BlockSpec((tm, a.shape[1]), lambda i, j: (i, 0), **lhs_mode) for a in a_list]
    in_specs += [pl.BlockSpec((w.shape[0], tn), lambda i, j: (0, j)) for w in w_list]
    in_specs.append(pl.BlockSpec((tm, tn), lambda i, j: (i, j)))
    if gate.shape[0] == 1:
        in_specs.append(pl.BlockSpec((1, tn), lambda i, j: (0, j)))
    else:
        in_specs.append(pl.BlockSpec((tm, tn), lambda i, j: (i, j)))
    return pl.pallas_call(
        functools.partial(_mm_res_kernel, n_lhs=len(a_list), coef=coef),
        grid=(M // tm, N // tn),
        in_specs=in_specs,
        out_specs=pl.BlockSpec((tm, tn), lambda i, j: (i, j)),
        out_shape=jax.ShapeDtypeStruct((M, N), F32),
        compiler_params=_cparams(("parallel", "parallel")),
        name=name,
    )(*a_list, *w_list, x, gate)


class _Layout:
    def __init__(self, ha, kv, hb, hd, di):
        self.ha, self.kv, self.hb, self.hd, self.di = ha, kv, hb, hd, di
        self.q_a = 0
        self.k_a = self.q_a + ha * hd
        self.v_a = self.k_a + kv * hd
        self.q_i = self.v_a + kv * hd
        self.kiw = self.q_i + N_IDX_HEADS * di
        self.q_b = self.kiw + LANES
        self.k_b = self.q_b + hb * 2 * hd
        self.v_b = self.k_b + hb * 2 * hd
        self.end = self.v_b + hb * 2 * hd
        self.width = _round_up(self.end, 512)


def _rope_tables(hd, di):
    def table(d):
        r = d // ROT_FRAC
        half = r // 2
        inv = ROPE_THETA ** (-(jnp.arange(half, dtype=jnp.float32) * 2.0 / r))
        lane = np.arange(LANES) % d
        idx = np.where(lane < half, lane, np.where(lane < r, lane - half, 0))
        invl = jnp.where(lane < r, inv[idx], 0.0)
        sign = np.where(lane < half, -1.0, np.where(lane < r, 1.0, 0.0)).astype(np.float32)
        src = np.where(lane < half, np.arange(LANES) + half, np.arange(LANES) - half)
        partner = np.zeros((LANES, LANES), np.float32)
        partner[src[lane < r], np.arange(LANES)[lane < r]] = 1.0
        return jnp.stack([invl, jnp.asarray(sign)])[:, None, :], jnp.asarray(partner, dtype=BF16)
    th, ph = table(hd)
    ti, pi = table(di)
    return th, ph, ti, pi


def _split3(x):
    x1 = x.astype(BF16)
    r1 = x - x1.astype(F32)
    x2 = r1.astype(BF16)
    x3 = (r1 - x2.astype(F32)).astype(BF16)
    return x1, x2, x3


def _post_kernel(z_ref, gqa_ref, gka_ref, gki_ref, gqb_ref, gkb_ref, th_ref, ph_ref, ti_ref, pi_ref,
                 ka_ref, va_ref, ki_ref, kb_ref, vb_ref,
                 qa16_ref, ka16_ref, va16_ref, qi16_ref, kiw_ref, kiw16_ref, qb16_ref, kb16_ref, vb16_ref,
                 *, lay, ts, pos_off, scale):
    hd, di = lay.hd, lay.di
    row = lax.broadcasted_iota(jnp.int32, (ts, LANES), 0)
    pos = (pos_off + pl.program_id(1) * ts + row).astype(F32)

    def rope_fn(t_ref, p_ref):
        ang = pos * t_ref[0]
        cosv = jnp.cos(ang)
        sinv = jnp.sin(ang) * t_ref[1]

        def rope(x):
            partner = sum(jnp.dot(t, p_ref[...], preferred_element_type=F32) for t in _split3(x))
            return x * cosv + partner * sinv
        return rope

    rope_h = rope_fn(th_ref, ph_ref)
    rope_i = rope_fn(ti_ref, pi_ref)

    def hnorm(x, g):
        ms = jnp.mean(x * x, axis=-1, keepdims=True)
        return x * lax.rsqrt(ms + EPS) * g

    def slab(off, h):
        return z_ref[0, :, off + h * hd: off + (h + 1) * hd]

    def cols(h):
        return slice(h * hd, (h + 1) * hd)

    for h in range(lay.ha):
        q = rope_h(hnorm(slab(lay.q_a, h), gqa_ref[...]))
        qa16_ref[0, :, cols(h)] = (q * scale).astype(BF16)
    for h in range(lay.kv):
        k = rope_h(hnorm(slab(lay.k_a, h), gka_ref[...]))
        ka_ref[0, :, cols(h)] = k
        ka16_ref[0, :, cols(h)] = k.astype(BF16)
        v = slab(lay.v_a, h)
        va_ref[0, :, cols(h)] = v
        va16_ref[0, :, cols(h)] = v.astype(BF16)
    for h in range(N_IDX_HEADS * di // LANES):
        qi16_ref[0, :, cols(h)] = rope_i(slab(lay.q_i, h)).astype(BF16)
    x = slab(lay.kiw, 0)
    lane = lax.broadcasted_iota(jnp.int32, (ts, LANES), 1)
    is_k = lane < di
    ms = jnp.sum(jnp.where(is_k, x * x, 0.0), axis=-1, keepdims=True) * (1.0 / di)
    ki = rope_i(x * lax.rsqrt(ms + EPS) * gki_ref[...])
    ki_ref[0] = ki[:, :di]
    kiw = jnp.where(is_k, ki, x)
    kiw_ref[0] = kiw
    kiw16_ref[0] = kiw.astype(BF16)
    for h in range(2 * lay.hb):
        q = rope_h(hnorm(slab(lay.q_b, h), gqb_ref[...]))
        qb16_ref[0, :, cols(h)] = (q * scale).astype(BF16)
        k = rope_h(hnorm(slab(lay.k_b, h), gkb_ref[...]))
        kb_ref[0, :, cols(h)] = k
        kb16_ref[0, :, cols(h)] = k.astype(BF16)
        v = slab(lay.v_b, h)
        vb_ref[0, :, cols(h)] = v
        vb16_ref[0, :, cols(h)] = v.astype(BF16)


def _post(z, lay, gains, tables, pos_off):
    B, S, W = z.shape
    hd, di = lay.hd, lay.di
    ts = _pick(S, (128, 64, 32, 16, 8))
    wa, wkv, wqi, wb = lay.ha * hd, lay.kv * hd, N_IDX_HEADS * di, lay.hb * 2 * hd

    def rows(width):
        return pl.BlockSpec((1, ts, width), lambda b, s: (b, s, 0))

    def const(a):
        return pl.BlockSpec(a.shape, lambda b, s: (0,) * a.ndim)

    out_widths = [(wkv, F32), (wkv, F32), (di, F32), (wb, F32), (wb, F32),
                  (wa, BF16), (wkv, BF16), (wkv, BF16), (wqi, BF16), (LANES, F32), (LANES, BF16),
                  (wb, BF16), (wb, BF16), (wb, BF16)]
    return pl.pallas_call(
        functools.partial(_post_kernel, lay=lay, ts=ts, pos_off=pos_off,
                          scale=hd ** -0.5 * math.log2(math.e)),
        grid=(B, S // ts),
        in_specs=[rows(W)] + [const(g) for g in gains] + [const(t) for t in tables],
        out_specs=[rows(w) for w, _ in out_widths],
        out_shape=[jax.ShapeDtypeStruct((B, S, w), dt) for w, dt in out_widths],
        compiler_params=_cparams(("parallel", "parallel")),
        name="head_norm_rope",
    )(z, *gains, *tables)


def _tile_bounds(p0, tq, tk, n_keys):
    a_full = jnp.minimum((p0 // CHUNK + 1) * CHUNK, n_keys)
    a_tot = jnp.minimum(((p0 + tq - 1) // CHUNK + 1) * CHUNK, n_keys)
    return a_full // tk, (a_tot + tk - 1) // tk


def _allowed(j, p0, tq, tk, n_keys):
    shift = CHUNK.bit_length() - 1
    kpos = j * tk + lax.broadcasted_iota(jnp.int32, (tq, tk), 1)
    qpos = p0 + lax.broadcasted_iota(jnp.int32, (tq, tk), 0)
    return (lax.shift_right_logical(kpos, shift) <= lax.shift_right_logical(qpos, shift)) & (kpos < n_keys)


def _lane_tile_list(parts):
    return parts[0] if len(parts) == 1 else jnp.concatenate(parts, axis=1)


def _lane_tile(x, reps):
    return _lane_tile_list([x] * reps)


KEY_TILE = 256
KEY_TILE_WIDE = 1024


def _key_tile(lp):
    return KEY_TILE_WIDE if lp % KEY_TILE_WIDE == 0 else KEY_TILE


def _lane_tile_max(s):
    out = s[:, :LANES]
    for c in range(1, s.shape[1] // LANES):
        out = jnp.maximum(out, s[:, c * LANES:(c + 1) * LANES])
    return out


def _softmax_step(s, v, m_ref, l_ref, acc_ref, idx, tile_max=None):
    tk = s.shape[1]
    wv = acc_ref.shape[-1]
    m_old = m_ref[idx]
    if tile_max is None:
        tile_max = _lane_tile_max(s)
    m_new = jnp.maximum(m_old, jnp.max(tile_max, axis=1, keepdims=True))
    alpha = jnp.exp2(m_old - m_new)
    p = jnp.exp2(s - _lane_tile(m_new, tk // LANES))
    if l_ref is not None:
        l_ref[idx] = alpha * l_ref[idx] + jnp.sum(p, axis=1, keepdims=True)
    acc_ref[idx] = _lane_tile(alpha, wv // LANES) * acc_ref[idx] + jnp.dot(
        p.astype(BF16), v, preferred_element_type=F32)
    m_ref[idx] = m_new


def _dsa_kernel(qa_ref, qi_ref, wq_ref, ka_ref, va_ref, ki_ref, rank_ref, o_ref,
                keys_ref, wexp_ref, qih_ref, tau_ref, quota_ref, seen_ref, bias_ref, m_ref, acc_ref, s_ref, mx_ref,
                *, tq, tk, n_keys, pos_off, topk, ha, kv, hd, di):
    p0 = pos_off + pl.program_id(1) * tq
    n_full, n_tot = _tile_bounds(p0, tq, tk, n_keys)
    reps = tk // LANES
    group = ha // kv
    int_min = jnp.int32(-2 ** 31)

    def rows(r):
        return slice(r * tq, (r + 1) * tq)

    lane = lax.broadcasted_iota(jnp.int32, (tq, LANES), 1)
    w_scale = (N_IDX_HEADS ** -0.5) * (di ** -0.5)
    for h in range(N_IDX_HEADS):
        pair = qi_ref[0, :, (h * di // LANES) * LANES:(h * di // LANES + 1) * LANES].astype(F32)
        off = (h * di) % LANES
        if off:
            pair = pltpu.roll(pair, LANES - off, 1)
        qih_ref[h // IDX_STACK, rows(h % IDX_STACK), :] = jnp.where(lane < di, pair, 0.0).astype(BF16)
        wexp_ref[h] = jnp.broadcast_to(wq_ref[0, :, di + h:di + h + 1] * w_scale, (tq, LANES))

    def key_slice(j):
        return pl.ds(pl.multiple_of(j * tk, tk), tk)

    def score_tile(j, masked):
        kt = ki_ref[0, key_slice(j), :]
        parts = [jnp.zeros((tq, LANES), F32)] * reps
        for hs in range(N_IDX_HEADS // IDX_STACK):
            lg = lax.dot_general(qih_ref[hs], kt, NT_DIMS, preferred_element_type=F32)
            for r in range(IDX_STACK):
                w = wexp_ref[hs * IDX_STACK + r]
                parts = [parts[c] + jnp.maximum(lg[rows(r), c * LANES:(c + 1) * LANES], 0.0) * w
                         for c in range(reps)]
        sc = _lane_tile_list(parts)
        if masked:
            sc = jnp.where(_allowed(j, p0, tq, tk, n_keys), sc, -jnp.inf)
        bits = pltpu.bitcast(sc, jnp.int32)
        keys_ref[j] = jnp.where(bits < 0, bits ^ jnp.int32(0x7FFFFFFF), bits)

    def score_full(j, c):
        score_tile(j, False)
        return c

    def score_masked(j, c):
        score_tile(j, True)
        return c

    lax.fori_loop(0, n_full, score_full, 0)
    lax.fori_loop(n_full, n_tot, score_masked, 0)

    def count_ge(cand):
        def body(j, acc):
            t = keys_ref[j]
            for c in range(reps):
                acc = jnp.where(t[:, c * LANES:(c + 1) * LANES] >= cand, acc + 1.0, acc)
            return acc
        acc = lax.fori_loop(0, n_tot, body, jnp.zeros((tq, LANES), F32))
        return jnp.sum(acc, axis=1, keepdims=True)

    kf = float(topk)
    zero = jnp.zeros((tq, LANES), jnp.int32)
    c0 = jnp.broadcast_to(count_ge(zero), (tq, LANES))
    n_all = jnp.broadcast_to((n_tot * tk).astype(F32), (tq, LANES))
    state = (jnp.int32(0), jnp.where(c0 >= kf, zero, int_min), jnp.where(c0 >= kf, c0, n_all))

    def bit_cond(state):
        b, _, cnt = state
        return (b < 31) & (jnp.max(jnp.where(cnt == kf, 0.0, 1.0)) > 0.0)

    def bit_body(state):
        b, prefix, cnt = state
        cand = prefix + lax.shift_left(jnp.int32(1), 30 - b)
        c = jnp.broadcast_to(count_ge(cand), (tq, LANES))
        return b + 1, jnp.where(c >= kf, cand, prefix), jnp.where(c >= kf, c, cnt)

    _, tau, n_ge = lax.while_loop(bit_cond, bit_body, state)
    tau_ref[...] = tau

    m_ref[...] = jnp.full(m_ref.shape, NEG, F32)
    acc_ref[...] = jnp.zeros(acc_ref.shape, F32)
    ones = jnp.ones((tk, hd), BF16)

    def selection_bias(j, tied):
        key = keys_ref[j]
        if not tied:
            return jnp.where(key >= _lane_tile(tau_ref[...], reps), 0.0, NEG)
        tau_t = tau_ref[...]
        quota = quota_ref[...]
        seen = seen_ref[...]
        parts = []
        for c in range(reps):
            kc = key[:, c * LANES:(c + 1) * LANES]
            tie = kc == tau_t
            cnt = jnp.dot(jnp.where(tie, 1.0, 0.0).astype(BF16), rank_ref[...], preferred_element_type=F32)
            sel = (kc > tau_t) | (tie & (seen + cnt[:, :LANES] < quota))
            parts.append(jnp.where(sel, 0.0, NEG))
            seen = seen + cnt[:, LANES:]
        seen_ref[...] = seen
        return _lane_tile_list(parts)

    def attn_tile(j, masked, tied):
        bias = selection_bias(j, tied)
        if masked:
            bias = jnp.where(_allowed(j, p0, tq, tk, n_keys), bias, NEG)
        bias_ref[...] = bias
        def produce(h):
            k = ka_ref[0, key_slice(j), (h // group) * hd:(h // group + 1) * hd]
            s = lax.dot_general(qa_ref[0, :, h * hd:(h + 1) * hd], k, NT_DIMS, preferred_element_type=F32)
            s = s + bias_ref[...]
            s_ref[h % 2] = s
            mx_ref[h % 2] = _lane_tile_max(s)

        def consume(h):
            g = h // group
            v1 = jnp.concatenate([va_ref[0, key_slice(j), g * hd:(g + 1) * hd], ones], axis=1)
            _softmax_step(s_ref[h % 2], v1, m_ref, None, acc_ref, h, tile_max=mx_ref[h % 2])

        produce(0)
        for h in range(ha):
            if h + 1 < ha:
                produce(h + 1)
            consume(h)

    def attend(tied):
        def attn_full(j, c):
            attn_tile(j, False, tied)
            return c

        def attn_masked(j, c):
            attn_tile(j, True, tied)
            return c

        lax.fori_loop(0, n_full, attn_full, 0)
        lax.fori_loop(n_full, n_tot, attn_masked, 0)

    any_tied = jnp.max(jnp.where(n_ge > kf, 1.0, 0.0)) > 0.0

    @pl.when(any_tied)
    def _():
        quota_ref[...] = kf - jnp.broadcast_to(count_ge(tau_ref[...] + 1), (tq, LANES))
        seen_ref[...] = jnp.zeros((tq, LANES), F32)
        attend(True)

    @pl.when(jnp.logical_not(any_tied))
    def _():
        attend(False)
    for h in range(ha):
        o_ref[0, :, h * hd:(h + 1) * hd] = (acc_ref[h, :, :hd] / acc_ref[h, :, hd:]).astype(o_ref.dtype)


def _dsa(qa, qi, kiw_q, ka, va, kiw_k, n_keys, pos_off, kv, hd, di):
    B, S, wa = qa.shape
    lp = ka.shape[1]
    ha = wa // hd
    tq = _pick(S, (128, 64, 32, 16, 8))
    tk = _key_tile(lp)
    topk = min(TOPK_MAX, n_keys // 4)
    kern = functools.partial(_dsa_kernel, tq=tq, tk=tk, n_keys=n_keys, pos_off=pos_off, topk=topk,
                             ha=ha, kv=kv, hd=hd, di=di)

    def rows(width):
        return pl.BlockSpec((1, tq, width), lambda b, i: (b, i, 0))

    def resident(width):
        return pl.BlockSpec((1, lp, width), lambda b, i: (b, 0, 0), pipeline_mode=pl.Buffered(1))

    lane = np.arange(LANES)
    rank_mat = jnp.asarray(np.concatenate([(lane[:, None] < lane[None, :]).astype(np.float32),
                                           np.ones((LANES, LANES), np.float32)], axis=1), dtype=BF16)
    return pl.pallas_call(
        kern,
        grid=(B, S // tq),
        in_specs=[rows(wa), rows(qi.shape[2]), rows(LANES),
                  resident(kv * hd), resident(kv * hd), resident(LANES),
                  pl.BlockSpec((LANES, 2 * LANES), lambda b, i: (0, 0))],
        out_specs=rows(wa),
        out_shape=jax.ShapeDtypeStruct((B, S, wa), BF16),
        scratch_shapes=[pltpu.VMEM((lp // tk, tq, tk), jnp.int32),
                        pltpu.VMEM((N_IDX_HEADS, tq, LANES), F32),
                        pltpu.VMEM((N_IDX_HEADS // IDX_STACK, IDX_STACK * tq, LANES), BF16),
                        pltpu.VMEM((tq, LANES), jnp.int32),
                        pltpu.VMEM((tq, LANES), F32),
                        pltpu.VMEM((tq, LANES), F32),
                        pltpu.VMEM((tq, tk), F32),
                        pltpu.VMEM((ha, tq, LANES), F32),
                        pltpu.VMEM((ha, tq, 2 * hd), F32),
                        pltpu.VMEM((2, tq, tk), F32),
                        pltpu.VMEM((2, tq, LANES), F32)],
        compiler_params=_cparams(("parallel", "arbitrary")),
        name="dsa_attention",
    )(qa, qi, kiw_q, ka, va, kiw_k, rank_mat)


def _diff_kernel(q_ref, k_ref, v_ref, lq1_ref, lk1_ref, lq2_ref, lk2_ref, sg_ref, o_ref,
                 m_ref, l_ref, acc_ref, s_ref, mx_ref, *, tq, tk, n_keys, pos_off, hd, lam_init):
    p0 = pos_off + pl.program_id(2) * tq
    n_full, n_tot = _tile_bounds(p0, tq, tk, n_keys)
    db = 2 * hd
    m_ref[...] = jnp.full(m_ref.shape, NEG, F32)
    l_ref[...] = jnp.zeros(l_ref.shape, F32)
    acc_ref[...] = jnp.zeros(acc_ref.shape, F32)

    def key_slice(j):
        return pl.ds(pl.multiple_of(j * tk, tk), tk)

    def scores(j, c):
        return lax.dot_general(q_ref[0, :, c * hd:(c + 1) * hd], k_ref[0, key_slice(j), c * hd:(c + 1) * hd],
                               NT_DIMS, preferred_element_type=F32)

    def produce(j, slot, masked=False):
        ok = _allowed(j, p0, tq, tk, n_keys) if masked else None
        for c in range(2):
            s = scores(j, c)
            if masked:
                s = jnp.where(ok, s, NEG)
            s_ref[slot, c] = s
            mx_ref[slot, c] = _lane_tile_max(s)

    def consume(j, slot):
        v = v_ref[0, key_slice(j), :]
        for c in range(2):
            _softmax_step(s_ref[slot, c], v, m_ref, l_ref, acc_ref, c, tile_max=mx_ref[slot, c])

    n_pipe = jnp.minimum(n_full + 1, n_tot)
    n_pairs = jnp.maximum(n_pipe - 2, 0) // 2
    first = 2 * n_pairs
    left = n_pipe - first

    @pl.when(n_pipe >= 2)
    def _():
        produce(0, 0)

    def pair_body(t, carry):
        produce(2 * t + 1, 1)
        consume(2 * t, 0)
        produce(2 * t + 2, 0)
        consume(2 * t + 1, 1)
        return carry

    lax.fori_loop(0, n_pairs, pair_body, 0)

    @pl.when(left == 1)
    def _():
        produce(first, 0, masked=True)
        consume(first, 0)

    @pl.when(left == 2)
    def _():
        produce(first + 1, 1, masked=True)
        consume(first, 0)
        consume(first + 1, 1)

    @pl.when(left == 3)
    def _():
        produce(first + 1, 1)
        consume(first, 0)
        produce(first + 2, 0, masked=True)
        consume(first + 1, 1)
        consume(first + 2, 0)

    def tile_masked(j, carry):
        produce(j, 0, masked=True)
        consume(j, 0)
        return carry

    lax.fori_loop(n_pipe, n_tot, tile_masked, 0)

    lam = (jnp.exp(jnp.sum(lq1_ref[...] * lk1_ref[...], axis=-1, keepdims=True))
           - jnp.exp(jnp.sum(lq2_ref[...] * lk2_ref[...], axis=-1, keepdims=True)) + lam_init)
    reps = db // LANES
    o = acc_ref[0] / _lane_tile(l_ref[0], reps) - lam * (acc_ref[1] / _lane_tile(l_ref[1], reps))
    ms = jnp.mean(o * o, axis=-1, keepdims=True)
    o_ref[0] = (o * lax.rsqrt(ms + EPS) * sg_ref[...] * (1.0 - lam_init)).astype(o_ref.dtype)


def _diff(qb, kb, vb, lams, subln_g, n_keys, pos_off, hd, lam_init):
    B, S, wb = qb.shape
    lp = kb.shape[1]
    db = 2 * hd
    hb = wb // db
    tq = _pick(S, (512, 256, 128, 64, 32, 16, 8))
    tk = _key_tile(lp)
    kern = functools.partial(_diff_kernel, tq=tq, tk=tk, n_keys=n_keys, pos_off=pos_off, hd=hd,
                             lam_init=lam_init)
    vec = pl.BlockSpec((1, hd), lambda b, h, i: (0, 0))
    return pl.pallas_call(
        kern,
        grid=(B, hb, S // tq),
        in_specs=[pl.BlockSpec((1, tq, db), lambda b, h, i: (b, i, h)),
                  pl.BlockSpec((1, lp, db), lambda b, h, i: (b, 0, h)),
                  pl.BlockSpec((1, lp, db), lambda b, h, i: (b, 0, h)),
                  vec, vec, vec, vec,
                  pl.BlockSpec((1, db), lambda b, h, i: (0, 0))],
        out_specs=pl.BlockSpec((1, tq, db), lambda b, h, i: (b, i, h)),
        out_shape=jax.ShapeDtypeStruct((B, S, wb), BF16),
        scratch_shapes=[pltpu.VMEM((2, tq, LANES), F32),
                        pltpu.VMEM((2, tq, LANES), F32),
                        pltpu.VMEM((2, tq, db), F32),
                        pltpu.VMEM((2, 2, tq, tk), F32),
                        pltpu.VMEM((2, 2, tq, LANES), F32)],
        compiler_params=_cparams(("parallel", "parallel", "arbitrary")),
        name="diff_attention",
    )(qb, kb, vb, *lams, subln_g)


def _pad_rows(a, lp):
    return jnp.pad(a, ((0, 0), (0, lp - a.shape[1]), (0, 0)))


def _prep_w_in(w_in, lay, d_idx_w):
    ha, kv, hb, hd, di = lay.ha, lay.kv, lay.hb, lay.hd, lay.di
    widths = (ha * hd, kv * hd, kv * hd, N_IDX_HEADS * di, di, d_idx_w, hb * 2 * hd, hb * 2 * hd, hb * 2 * hd)
    offs = np.cumsum((0,) + widths)
    seg = [w_in[:, offs[i]:offs[i + 1]] for i in range(len(widths))]
    d = w_in.shape[0]
    pad_kiw = jnp.zeros((d, LANES - di - d_idx_w), w_in.dtype)
    pad_end = jnp.zeros((d, lay.width - lay.end), w_in.dtype)
    return jnp.concatenate(seg[:6] + [pad_kiw] + seg[6:] + [pad_end], axis=1).astype(BF16)


def _layer(x, B, S, mods, past, pos_off, lw, li):
    (sh1, sc1, gt1, sh2, sc2, gt2, sh3, sc3, gt3) = mods
    D = x.shape[1]
    lay = lw["lay"]
    hd, di, kv = lay.hd, lay.di, lay.kv

    h = _norm_mod(x, lw["norm_ffn1"], sc1, sh1)
    a = _mm_up(h, lw["w1_1"], lw["w3_1"])
    x = _mm_res([a], [lw["w2_1"]], x, gt1, 0.5, "ffn_down")

    h = _norm_mod(x, lw["norm_attn"], sc2, sh2)
    z = _mm_plain(h, lw["w_in"]).reshape(B, S, lay.width)
    (k_a, v_a, k_i, k_b, v_b, qa16, ka16, va16, qi16, kiw, kiw16, qb16, kb16, vb16) = _post(
        z, lay, lw["gains"], lw["tables"], pos_off)

    if past is not None:
        ck_a, cv_a, ck_i, ck_b, cv_b = past
        P = ck_a.shape[1]
        flat = lambda c: c.reshape(B, P, -1).astype(BF16)
        ka16 = jnp.concatenate([flat(ck_a), ka16], axis=1)
        va16 = jnp.concatenate([flat(cv_a), va16], axis=1)
        ck_i16 = jnp.pad(flat(ck_i), ((0, 0), (0, 0), (0, LANES - di)))
        kiw16_k = jnp.concatenate([ck_i16, kiw16], axis=1)
        kb16 = jnp.concatenate([flat(ck_b), kb16], axis=1)
        vb16 = jnp.concatenate([flat(cv_b), vb16], axis=1)
        n_keys = P + S
    else:
        kiw16_k = kiw16
        n_keys = S
    lp = _round_up(n_keys, KEY_TILE)
    if lp != n_keys:
        ka16, va16, kiw16_k, kb16, vb16 = (_pad_rows(t, lp) for t in (ka16, va16, kiw16_k, kb16, vb16))

    oa = _dsa(qa16, qi16, kiw, ka16, va16, kiw16_k, n_keys, pos_off, kv, hd, di)
    ob = _diff(qb16, kb16, vb16, lw["lams"], lw["subln_g"], n_keys, pos_off, hd, _lambda_init(li))
    wa = oa.shape[2]
    x = _mm_res([oa.reshape(B * S, wa), ob.reshape(B * S, -1)], [lw["w_out"][:wa], lw["w_out"][wa:]],
                x, gt2, 1.0, "out_proj")

    h = _norm_mod(x, lw["norm_ffn2"], sc3, sh3)
    a = _mm_up(h, lw["w1_2"], lw["w3_2"])
    x = _mm_res([a], [lw["w2_2"]], x, gt3, 0.5, "ffn_down")
    return x, (k_a, v_a, k_i, k_b, v_b)


def kernel(x_prompt, x_sample, cache_k_a, cache_v_a, cache_k_idx, cache_k_b, cache_v_b, c_prompt, c_sample, w_ada, b_ada, norm_ffn1, w1_ffn1, w3_ffn1, w2_ffn1, norm_attn, w_in, qnorm_a, knorm_a, knorm_idx, qnorm_b, knorm_b, lam_q1, lam_k1, lam_q2, lam_k2, subln_b, w_out, norm_ffn2, w1_ffn2, w3_ffn2, w2_ffn2):
    depth = w_ada.shape[0]
    Bp, Sp, D = x_prompt.shape
    Bs, Ss, _ = x_sample.shape
    P = cache_k_a.shape[2]
    kv, hd = cache_k_a.shape[3], cache_k_a.shape[4]
    di = cache_k_idx.shape[3]
    hb = cache_k_b.shape[3]
    db = cache_v_b.shape[4]
    ha = (w_out.shape[1] - hb * db) // hd
    lay = _Layout(ha, kv, hb, hd, di)
    tables = _rope_tables(hd, di)

    xp = x_prompt.reshape(Bp * Sp, D)
    xs = x_sample.reshape(Bs * Ss, D)
    n_c = Bp + Bs
    c_all = jnp.pad(jnp.concatenate([c_prompt, c_sample], axis=0), ((0, _round_up(n_c, 8) - n_c), (0, 0)))

    rows_p, rows_s = [], []
    for l in range(depth):
        row = lambda a: a[l][None, :]
        gki = jnp.pad(knorm_idx[l], (0, LANES - di))[None, :]
        lw = dict(
            lay=lay, tables=tables,
            norm_ffn1=row(norm_ffn1), norm_attn=row(norm_attn), norm_ffn2=row(norm_ffn2),
            w1_1=w1_ffn1[l], w3_1=w3_ffn1[l], w2_1=w2_ffn1[l].astype(BF16),
            w1_2=w1_ffn2[l], w3_2=w3_ffn2[l], w2_2=w2_ffn2[l].astype(BF16),
            w_in=_prep_w_in(w_in[l], lay, N_IDX_HEADS), w_out=w_out[l].astype(BF16),
            gains=(row(qnorm_a), row(knorm_a), gki, row(qnorm_b), row(knorm_b)),
            lams=(row(lam_q1), row(lam_k1), row(lam_q2), row(lam_k2)),
            subln_g=row(subln_b),
        )
        mod = _ada(c_all, w_ada[l], b_ada[l][None, :])

        def mods_for(lo, nb, s):
            m = mod[lo:lo + nb].reshape(nb, N_MOD, D)
            if nb == 1:
                return tuple(m[:, i] for i in range(N_MOD))
            return tuple(jnp.repeat(m[:, i], s, axis=0) for i in range(N_MOD))

        xp, rp = _layer(xp, Bp, Sp, mods_for(0, Bp, Sp), None, 0, lw, l)
        past = (cache_k_a[l], cache_v_a[l], cache_k_idx[l], cache_k_b[l], cache_v_b[l])
        xs, rs = _layer(xs, Bs, Ss, mods_for(Bp, Bs, Ss), past, P, lw, l)
        rows_p.append(rp)
        rows_s.append(rs)

    def stack(rows, idx, B, S, tail):
        return jnp.stack([r[idx].reshape((B, S) + tail) for r in rows])

    tails = ((kv, hd), (kv, hd), (di,), (hb, 2, hd), (hb, db))
    outs_p = tuple(stack(rows_p, i, Bp, Sp, tails[i]) for i in range(5))
    outs_s = tuple(stack(rows_s, i, Bs, Ss, tails[i]) for i in range(5))
    return (xp.reshape(Bp, Sp, D), xs.reshape(Bs, Ss, D)) + outs_p + outs_s
```
